```python
import math
import jax, jax.numpy as jnp
from jax import lax
import numpy as np

D_MODEL = 2048
BATCH = 4
SEQ = 4096
DEPTH = 1

MIX_WIDTH = D_MODEL
DIFF_HEADS = 8
DIFF_HEAD_DIM = 64
DIFF_V_DIM = 2 * DIFF_HEAD_DIM
DIFF_WIDTH = DIFF_HEADS * DIFF_V_DIM
RET_HEADS = 8
RET_QK_DIM = 128
RET_V_DIM = 128
RET_WIDTH = RET_HEADS * RET_V_DIM
RET_CHUNK = 128
Q_BLOCK = 128
ROPE_THETA = 10000.0
N_EXPERTS = 16
EXPERT_FF = 2048
EC_CAPACITY_FACTOR = 2
NORM_EPS = 1e-6
N_MOD = 6
SPLITS = (
    DIFF_HEADS * 2 * DIFF_HEAD_DIM,
    DIFF_HEADS * 2 * DIFF_HEAD_DIM,
    DIFF_WIDTH,
    RET_HEADS * RET_QK_DIM,
    RET_HEADS * RET_QK_DIM,
    RET_WIDTH,
    RET_WIDTH,
)
IN_COLS = sum(SPLITS)

kernel_name = "hybrid_diffattn_retention_ec_moe_block"


def _lambda_init(layer_idx):
    return 0.8 - 0.6 * math.exp(-0.3 * layer_idx)


def _rmsnorm(x, gain):
    xf = x.astype(jnp.float32)
    y = xf * lax.rsqrt(jnp.mean(xf * xf, axis=-1, keepdims=True) + NORM_EPS)
    return (y * gain.astype(jnp.float32)).astype(x.dtype)


def _head_layernorm(x, gain):
    xf = x.astype(jnp.float32)
    mu = jnp.mean(xf, axis=-1, keepdims=True)
    var = jnp.mean(jnp.square(xf - mu), axis=-1, keepdims=True)
    y = (xf - mu) * lax.rsqrt(var + NORM_EPS)
    H, dv = x.shape[1], x.shape[3]
    g = gain.astype(jnp.float32).reshape(1, H, 1, dv)
    return (y * g).astype(x.dtype)


def _rope_tables(positions, dim):
    inv = 1.0 / (ROPE_THETA ** (jnp.arange(0, dim, 2, dtype=jnp.float32) / dim))
    ang = positions.astype(jnp.float32)[..., None] * inv
    return jnp.cos(ang), jnp.sin(ang)


def _rope(x, cos, sin):
    shape = cos.shape[:2] + (1,) * (x.ndim - 3) + cos.shape[-1:]
    cos = cos.reshape(shape)
    sin = sin.reshape(shape)
    x1, x2 = jnp.split(x.astype(jnp.float32), 2, axis=-1)
    return jnp.concatenate([x1 * cos - x2 * sin, x2 * cos + x1 * sin], axis=-1).astype(x.dtype)


def _diff_attention(qs, ks, v, lam):
    B, H, _, S, d = qs.shape
    nb = S // Q_BLOCK
    scale = d ** -0.5
    qb = jnp.moveaxis(qs.reshape(B, H, 2, nb, Q_BLOCK, d), 3, 0)

    def block(q_blk):
        s = jnp.einsum('bhmqd,bhmkd->bhmqk', q_blk, ks).astype(jnp.float32) * scale
        p = jax.nn.softmax(s, axis=-1)
        w = p[:, :, 0] - lam * p[:, :, 1]
        return jnp.einsum('bhqk,bhke->bhqe', w.astype(v.dtype), v)

    out = lax.map(block, qb)
    return jnp.moveaxis(out, 0, 2).reshape(B, H, S, v.shape[-1])


def _retention_one_direction(q, k, v, gamma, strict):
    B, H, S, dk = q.shape
    dv = v.shape[-1]
    C = RET_CHUNK
    N = S // C
    dt = q.dtype
    qc = q.reshape(B, H, N, C, dk)
    kc = k.reshape(B, H, N, C, dk)
    vc = v.reshape(B, H, N, C, dv)
    log_g = jnp.log(gamma)
    j = jnp.arange(C, dtype=jnp.float32)
    dist = j[:, None] - j[None, :]
    mask = (dist > 0) if strict else (dist >= 0)
    inner_decay = jnp.where(mask[None], jnp.exp(log_g[:, None, None] * jnp.maximum(dist, 0.0)[None]), 0.0)
    scores = jnp.einsum('bhncd,bhnmd->bhncm', qc, kc) * inner_decay[None, :, None].astype(dt)
    inner = jnp.einsum('bhncm,bhnme->bhnce', scores, vc)
    k_decay = jnp.exp(log_g[:, None] * (C - 1 - j)[None]).astype(dt)
    kv = jnp.einsum('bhncd,bhnce->bhnde', kc * k_decay[None, :, None, :, None], vc)
    chunk_decay = jnp.exp(log_g * C).astype(kv.dtype)[None, :, None, None]

    def step(state, kv_n):
        return state * chunk_decay + kv_n, state

    init = jnp.zeros((B, H, dk, dv), kv.dtype)
    _, prev = lax.scan(step, init, jnp.moveaxis(kv, 2, 0))
    prev = jnp.moveaxis(prev, 0, 2)
    q_decay = jnp.exp(log_g[:, None] * (j + 1.0)[None]).astype(dt)
    cross = jnp.einsum('bhncd,bhnde->bhnce', qc * q_decay[None, :, None, :, None], prev)
    return (inner + cross).reshape(B, H, S, dv)


def _bidirectional_retention(q, k, v, gamma_fwd, gamma_bwd):
    fwd = _retention_one_direction(q, k, v, gamma_fwd, strict=False)
    bwd = _retention_one_direction(jnp.flip(q, 2), jnp.flip(k, 2), jnp.flip(v, 2), gamma_bwd, strict=True)
    return fwd + jnp.flip(bwd, 2)


def _expert_choice_ffn(h, w_router, w_gate, w_up, w_down):
    B, S, D = h.shape
    cap = EC_CAPACITY_FACTOR * S // N_EXPERTS
    logits = jnp.einsum('bsd,de->bse', h, w_router).astype(jnp.float32)
    aff = jax.nn.softmax(logits, axis=-1)
    gates, idx = lax.top_k(jnp.swapaxes(aff, 1, 2), cap)
    xin = jax.vmap(lambda hb, ib: hb[ib])(h, idx)
    a = jnp.einsum('becd,edf->becf', xin, w_gate)
    u = jnp.einsum('becd,edf->becf', xin, w_up)
    y = jnp.einsum('becf,efd->becd', jax.nn.silu(a) * u, w_down) * gates[..., None].astype(h.dtype)
    return jax.vmap(lambda yb, ib: jnp.zeros((S, D), h.dtype).at[ib.reshape(-1)].add(yb.reshape(-1, D)))(y, idx)


def setup_inputs(seed: int = 0) -> dict:
    key = jax.random.key(seed)
    ks = jax.random.split(key, 24)
    L, D, E, F = DEPTH, D_MODEL, N_EXPERTS, EXPERT_FF
    nrm = lambda k, shape, s: jax.random.normal(k, shape, jnp.float32) * s
    base = np.log2(1.0 - 2.0 ** (-5.0 - np.arange(RET_HEADS)))
    gam = 1.0 - 2.0 ** (-5.0 - np.arange(RET_HEADS, dtype=np.float32))
    logit = jnp.asarray(np.log(gam / (1.0 - gam)), jnp.float32)
    return {
        'x': nrm(ks[0], (BATCH, SEQ, D), 1.0),
        'c': nrm(ks[1], (BATCH, D), 1.0),
        'positions': jnp.broadcast_to(jnp.arange(SEQ, dtype=jnp.int32)[None], (BATCH, SEQ)),
        'w_ada': nrm(ks[2], (L, D, N_MOD * D), 0.5 * D ** -0.5),
        'b_ada': nrm(ks[3], (L, N_MOD * D), 0.01),
        'g_pre_mix': 1.0 + nrm(ks[4], (L, D), 0.02),
        'g_post_mix': 1.0 + nrm(ks[5], (L, D), 0.02),
        'w_in': nrm(ks[6], (L, D, IN_COLS), D ** -0.5),
        'diff_lambda_q1': nrm(ks[7], (L, DIFF_HEAD_DIM), 0.1),
        'diff_lambda_k1': nrm(ks[8], (L, DIFF_HEAD_DIM), 0.1),
        'diff_lambda_q2': nrm(ks[9], (L, DIFF_HEAD_DIM), 0.1),
        'diff_lambda_k2': nrm(ks[10], (L, DIFF_HEAD_DIM), 0.1),
        'diff_subln_gain': 1.0 + nrm(ks[11], (L, DIFF_V_DIM), 0.02),
        'ret_decay_logit': logit[None, None, :] + nrm(ks[12], (L, 2, RET_HEADS), 0.1),
        'ret_norm_gain': 1.0 + nrm(ks[13], (L, RET_WIDTH), 0.02),
        'w_out': nrm(ks[14], (L, MIX_WIDTH, D), MIX_WIDTH ** -0.5),
        'g_pre_ffn': 1.0 + nrm(ks[15], (L, D), 0.02),
        'g_post_ffn': 1.0 + nrm(ks[16], (L, D), 0.02),
        'w_router': nrm(ks[17], (L, D, E), D ** -0.5),
        'w_gate': nrm(ks[18], (L, E, D, F), D ** -0.5),
        'w_up': nrm(ks[19], (L, E, D, F), D ** -0.5),
        'w_down': nrm(ks[20], (L, E, F, D), F ** -0.5),
    }


def reference(x, c, positions, w_ada, b_ada, g_pre_mix, g_post_mix, w_in,
              diff_lambda_q1, diff_lambda_k1, diff_lambda_q2, diff_lambda_k2, diff_subln_gain,
              ret_decay_logit, ret_norm_gain, w_out, g_pre_ffn, g_post_ffn,
              w_router, w_gate, w_up, w_down):
    B, S, D = x.shape
    cos_d, sin_d = _rope_tables(positions, DIFF_HEAD_DIM)
    cos_r, sin_r = _rope_tables(positions, RET_QK_DIM)
    split_at = np.cumsum(SPLITS)[:-1].tolist()
    for l in range(DEPTH):
        lam_init = _lambda_init(l)
        mod = jnp.einsum('bd,de->be', jax.nn.silu(c), w_ada[l]) + b_ada[l]
        shift1, scale1, gate1, shift2, scale2, gate2 = jnp.split(mod[:, None, :], N_MOD, axis=-1)

        h = _rmsnorm(x, g_pre_mix[l]) * (1.0 + scale1) + shift1
        proj = jnp.einsum('bsd,dc->bsc', h, w_in[l])
        dq, dk, dv, rq, rk, rv, rg = jnp.split(proj, split_at, axis=-1)

        dq = _rope(dq.reshape(B, S, DIFF_HEADS, 2, DIFF_HEAD_DIM), cos_d, sin_d)
        dk = _rope(dk.reshape(B, S, DIFF_HEADS, 2, DIFF_HEAD_DIM), cos_d, sin_d)
        qs = dq.transpose(0, 2, 3, 1, 4)
        kss = dk.transpose(0, 2, 3, 1, 4)
        vd = dv.reshape(B, S, DIFF_HEADS, DIFF_V_DIM).transpose(0, 2, 1, 3)
        lam = (jnp.exp(jnp.sum(diff_lambda_q1[l].astype(jnp.float32) * diff_lambda_k1[l].astype(jnp.float32)))
               - jnp.exp(jnp.sum(diff_lambda_q2[l].astype(jnp.float32) * diff_lambda_k2[l].astype(jnp.float32)))
               + lam_init)
        ad = _diff_attention(qs, kss, vd, lam)
        ad = _rmsnorm(ad, diff_subln_gain[l]) * (1.0 - lam_init)
        ad = ad.transpose(0, 2, 1, 3).reshape(B, S, DIFF_WIDTH)

        rq = _rope(rq.reshape(B, S, RET_HEADS, RET_QK_DIM), cos_r, sin_r).transpose(0, 2, 1, 3)
        rk = _rope(rk.reshape(B, S, RET_HEADS, RET_QK_DIM), cos_r, sin_r).transpose(0, 2, 1, 3)
        rk = rk * (RET_QK_DIM ** -0.5)
        rvv = rv.reshape(B, S, RET_HEADS, RET_V_DIM).transpose(0, 2, 1, 3)
        gammas = jax.nn.sigmoid(ret_decay_logit[l].astype(jnp.float32))
        ro = _bidirectional_retention(rq, rk, rvv, gammas[0], gammas[1])
        ro = _head_layernorm(ro, ret_norm_gain[l]).transpose(0, 2, 1, 3).reshape(B, S, RET_WIDTH)
        ro = jax.nn.silu(rg) * ro

        mix = jnp.einsum('bsc,cd->bsd', jnp.concatenate([ad, ro], axis=-1), w_out[l])
        x = x + gate1 * _rmsnorm(mix, g_post_mix[l])

        h = _rmsnorm(x, g_pre_ffn[l]) * (1.0 + scale2) + shift2
        ff = _expert_choice_ffn(h, w_router[l], w_gate[l], w_up[l], w_down[l])
        x = x + gate2 * _rmsnorm(ff, g_post_ffn[l])
    return x
```

```python
import functools
import math

import jax
import jax.numpy as jnp
from jax import lax
from jax.experimental import pallas as pl
from jax.experimental.pallas import tpu as pltpu

F32 = jnp.float32
BF16 = jnp.bfloat16
HIGHEST = lax.Precision.HIGHEST

DIFF_HEADS = 8
DIFF_HEAD_DIM = 64
RET_HEADS = 8
RET_QK_DIM = 128
LANE = 128
N_EXPERTS = 16
EC_CAPACITY_FACTOR = 2
ROPE_THETA = 10000.0
NORM_EPS = 1e-6
RET_CHUNK = 128
VMEM_LIMIT = 56 * 1024 * 1024


def _cparams(sem):
    return pltpu.CompilerParams(dimension_semantics=sem, vmem_limit_bytes=VMEM_LIMIT)


def _silu(x):
    return x / (1.0 + jnp.exp(-x))


def _rms(x, eps=NORM_EPS):
    return x * lax.rsqrt(jnp.mean(x * x, axis=-1, keepdims=True) + eps)


def _ada_kernel(c_ref, w_ref, b_ref, o_ref):
    s = _silu(c_ref[...])
    o_ref[...] = jnp.dot(s, w_ref[...], precision=HIGHEST, preferred_element_type=F32) + b_ref[...]


def _ada(c_pad, w_ada, b_ada):
    m, d = c_pad.shape
    n = w_ada.shape[1]
    tn = min(d, 1024)
    assert n % tn == 0
    return pl.pallas_call(
        _ada_kernel,
        grid=(n // tn,),
        in_specs=[pl.BlockSpec((m, d), lambda j: (0, 0)),
                  pl.BlockSpec((d, tn), lambda j: (0, j)),
                  pl.BlockSpec((1, tn), lambda j: (0, j))],
        out_specs=pl.BlockSpec((m, tn), lambda j: (0, j)),
        out_shape=jax.ShapeDtypeStruct((m, n), F32),
        compiler_params=_cparams(("arbitrary",)),
        name="ada",
    )(c_pad, w_ada, b_ada)


def _rope_tab_kernel(pos_ref, cd_ref, sda_ref, sdb_ref, cr_ref, sr_ref):
    pos = pos_ref[...].astype(F32)
    lane = lax.broadcasted_iota(jnp.int32, (pos.shape[0], LANE), 1)
    is_d = lane < 32
    expo = jnp.where(is_d, lane.astype(F32) * (2.0 / DIFF_HEAD_DIM),
                     (lane - 32).astype(F32) * (2.0 / RET_QK_DIM))
    inv = jnp.exp(expo * (-math.log(ROPE_THETA)))
    ang = pos * inv
    c = jnp.cos(ang)
    s = jnp.sin(ang)
    c32, c64, c96 = pltpu.roll(c, 32, 1), pltpu.roll(c, 64, 1), pltpu.roll(c, 96, 1)
    s32, s64, s96 = pltpu.roll(s, 32, 1), pltpu.roll(s, 64, 1), pltpu.roll(s, 96, 1)
    q = lane // 32
    cos_d = jnp.where(q == 0, c, jnp.where(q == 1, c32, jnp.where(q == 2, c64, c96)))
    sin_d = jnp.where(q == 0, s, jnp.where(q == 1, s32, jnp.where(q == 2, s64, s96)))
    first_half = (q == 0) | (q == 2)
    cd_ref[...] = cos_d
    sda_ref[...] = jnp.where(first_half, -sin_d, 0.0)
    sdb_ref[...] = jnp.where(first_half, 0.0, sin_d)
    lo = lane < 64
    cr_ref[...] = jnp.where(lo, c96, c32)
    sr_ref[...] = jnp.where(lo, -s96, s32)


def _rope_tables(pos_col):
    t = pos_col.shape[0]
    tm = min(t, 2048)
    spec = pl.BlockSpec((tm, LANE), lambda i: (i, 0))
    return pl.pallas_call(
        _rope_tab_kernel,
        grid=(t // tm,),
        in_specs=[pl.BlockSpec((tm, 1), lambda i: (i, 0))],
        out_specs=[spec] * 5,
        out_shape=[jax.ShapeDtypeStruct((t, LANE), F32)] * 5,
        compiler_params=_cparams(("arbitrary",)),
        name="rope_tab",
    )(pos_col)


def _inproj_kernel(x_ref, sc_ref, sh_ref, g_ref, w_ref, cd_ref, sda_ref, sdb_ref, cr_ref, sr_ref,
                   o_ref, h_ref, *, blocks_per_type, q_scale, rk_scale):
    j = pl.program_id(1)

    @pl.when(j == 0)
    def _():
        h = _rms(x_ref[...]) * g_ref[...] * (1.0 + sc_ref[0]) + sh_ref[0]
        h_ref[...] = h.astype(BF16)

    acc = jnp.dot(h_ref[...], w_ref[...], preferred_element_type=F32)
    ctype = j // blocks_per_type
    nslab = acc.shape[1] // LANE

    def rope_d(scale):
        cd, sa, sb = cd_ref[...], sda_ref[...], sdb_ref[...]
        for s in range(nslab):
            xs = acc[:, s * LANE:(s + 1) * LANE]
            y = xs * cd + pltpu.roll(xs, 96, 1) * sa + pltpu.roll(xs, 32, 1) * sb
            o_ref[:, s * LANE:(s + 1) * LANE] = (y * scale).astype(BF16)

    def rope_r(scale):
        cr, sr = cr_ref[...], sr_ref[...]
        for s in range(nslab):
            xs = acc[:, s * LANE:(s + 1) * LANE]
            y = xs * cr + pltpu.roll(xs, 64, 1) * sr
            o_ref[:, s * LANE:(s + 1) * LANE] = (y * scale).astype(BF16)

    @pl.when(ctype == 0)
    def _():
        rope_d(q_scale)

    @pl.when(ctype == 1)
    def _():
        rope_d(1.0)

    @pl.when(ctype == 3)
    def _():
        rope_r(1.0)

    @pl.when(ctype == 4)
    def _():
        rope_r(rk_scale)

    @pl.when((ctype == 2) | (ctype >= 5))
    def _():
        o_ref[...] = acc.astype(BF16)


def _inproj(x2, scale1, shift1, g, w_bf, tabs, seq):
    t, d = x2.shape
    nc = w_bf.shape[1]
    type_w = nc // 7
    tm = min(seq, 1024)
    tn = min(type_w, 1024)
    tiles_per_seq = seq // tm
    kern = functools.partial(
        _inproj_kernel, blocks_per_type=type_w // tn,
        q_scale=(DIFF_HEAD_DIM ** -0.5) * math.log2(math.e), rk_scale=RET_QK_DIM ** -0.5)
    mod_spec = pl.BlockSpec((1, 1, d), lambda i, j: (i // tiles_per_seq, 0, 0))
    tab_spec = pl.BlockSpec((tm, LANE), lambda i, j: (i, 0))
    return pl.pallas_call(
        kern,
        grid=(t // tm, nc // tn),
        in_specs=[pl.BlockSpec((tm, d), lambda i, j: (i, 0)), mod_spec, mod_spec,
                  pl.BlockSpec((1, d), lambda i, j: (0, 0)),
                  pl.BlockSpec((d, tn), lambda i, j: (0, j))] + [tab_spec] * 5,
        out_specs=pl.BlockSpec((tm, tn), lambda i, j: (i, j)),
        out_shape=jax.ShapeDtypeStruct((t, nc), BF16),
        scratch_shapes=[pltpu.VMEM((tm, d), BF16)],
        compiler_params=_cparams(("arbitrary", "arbitrary")),
        name="inproj",
    )(x2, scale1, shift1, g, w_bf, *tabs)


def _diffattn_kernel(lq1_ref, lk1_ref, lq2_ref, lk2_ref, q_ref, k_ref, v_ref, g_ref, o_ref,
                     vt_ref, s_ref, *, lam_init, tk):
    @pl.when(pl.program_id(2) == 0)
    def _():
        vt_ref[...] = v_ref[0].astype(F32).T.astype(BF16)

    lam = (jnp.exp(jnp.sum(lq1_ref[...] * lk1_ref[...], axis=-1, keepdims=True))
           - jnp.exp(jnp.sum(lq2_ref[...] * lk2_ref[...], axis=-1, keepdims=True)) + lam_init)

    q = q_ref[0]
    tq = q.shape[0]
    s_len = k_ref.shape[1]
    nk = s_len // tk
    lane = lax.broadcasted_iota(jnp.int32, q.shape, 1)
    ones8 = jnp.ones((8, tk), BF16)
    outs = []
    for m in range(2):
        in_map = (lane >= m * DIFF_HEAD_DIM) & (lane < (m + 1) * DIFF_HEAD_DIM)
        qm = jnp.where(in_map, q, jnp.zeros_like(q))
        cmax = jnp.full((8, tq), -jnp.inf, F32)
        for c in range(nk):
            kc = k_ref[0, c * tk:(c + 1) * tk, :]
            s = lax.dot_general(kc, qm, (((1,), (1,)), ((), ())), preferred_element_type=F32)
            s_ref[c * tk:(c + 1) * tk, :] = s
            cmax = jnp.maximum(cmax, jnp.max(s.reshape(tk // 8, 8, tq), axis=0))
        mx = jnp.max(cmax, axis=0, keepdims=True)
        acc = jnp.zeros((vt_ref.shape[0], tq), F32)
        den = jnp.zeros((8, tq), F32)
        for c in range(nk):
            p = jnp.exp2(s_ref[c * tk:(c + 1) * tk, :] - mx).astype(BF16)
            acc = acc + jnp.dot(vt_ref[:, c * tk:(c + 1) * tk], p, preferred_element_type=F32)
            den = den + jnp.dot(ones8, p, preferred_element_type=F32)
        outs.append(acc / den[0:1, :])
    ad = outs[0] - lam * outs[1]
    y = ad * lax.rsqrt(jnp.mean(ad * ad, axis=0, keepdims=True) + NORM_EPS)
    o_ref[0] = (y.T * (g_ref[...] * (1.0 - lam_init))).astype(BF16)


def _diffattn(proj3, lams, subln_gain, lam_init):
    b, s, _ = proj3.shape
    h = DIFF_HEADS
    tq = min(s, 256)
    tk = min(s, 512)
    kern = functools.partial(_diffattn_kernel, lam_init=lam_init, tk=tk)
    lam_spec = pl.BlockSpec((1, DIFF_HEAD_DIM), lambda bi, hi, qi: (0, 0))
    return pl.pallas_call(
        kern,
        grid=(b, h, s // tq),
        in_specs=[lam_spec] * 4 + [
            pl.BlockSpec((1, tq, LANE), lambda bi, hi, qi: (bi, qi, hi)),
            pl.BlockSpec((1, s, LANE), lambda bi, hi, qi: (bi, 0, h + hi)),
            pl.BlockSpec((1, s, LANE), lambda bi, hi, qi: (bi, 0, 2 * h + hi)),
            pl.BlockSpec((1, LANE), lambda bi, hi, qi: (0, 0))],
        out_specs=pl.BlockSpec((1, tq, LANE), lambda bi, hi, qi: (bi, qi, hi)),
        out_shape=jax.ShapeDtypeStruct((b, s, h * LANE), BF16),
        scratch_shapes=[pltpu.VMEM((LANE, s), BF16), pltpu.VMEM((s, tq), F32)],
        compiler_params=_cparams(("arbitrary", "arbitrary", "arbitrary")),
        name="diffattn",
    )(*lams, proj3, proj3, proj3, subln_gain)


def _retention_kernel(logit_ref, q_ref, k_ref, v_ref, gate_ref, gain_ref, o_ref, acc_ref, *, chunk):
    hi = pl.program_id(1)
    c_len = chunk
    s_len = q_ref.shape[1]
    n_chunks = s_len // c_len
    lane = lax.broadcasted_iota(jnp.int32, (1, logit_ref.shape[1]), 1)
    lg = logit_ref[...]
    log_g = -jnp.log(1.0 + jnp.exp(-lg))
    log_g = jnp.sum(jnp.where(lane == hi, log_g, 0.0), axis=-1, keepdims=True)
    lgf, lgb = log_g[0:1, :], log_g[1:2, :]

    ii = lax.broadcasted_iota(jnp.int32, (c_len, c_len), 0).astype(F32)
    jj = lax.broadcasted_iota(jnp.int32, (c_len, c_len), 1).astype(F32)
    dist = ii - jj
    decay = jnp.where(dist >= 0, jnp.exp(lgf * jnp.maximum(dist, 0.0)),
                      jnp.exp(lgb * jnp.maximum(-dist, 0.0)))
    ci = lax.broadcasted_iota(jnp.int32, (c_len, 1), 0).astype(F32)
    qdec_f = jnp.exp(lgf * (ci + 1.0))
    kdec_f = jnp.exp(lgf * (c_len - 1.0 - ci))
    qdec_b = jnp.exp(lgb * (c_len - ci))
    kdec_b = jnp.exp(lgb * ci)
    cdec_f = jnp.exp(lgf * c_len)
    cdec_b = jnp.exp(lgb * c_len)
    dk = q_ref.shape[2]
    dv = v_ref.shape[2]

    def load(c):
        r = pl.ds(pl.multiple_of(c * c_len, c_len), c_len)
        return q_ref[0, r, :], k_ref[0, r, :], v_ref[0, r, :], r

    def fwd(c, state):
        q, k, v, r = load(c)
        qf, kf = q.astype(F32), k.astype(F32)
        sc = lax.dot_general(q, k, (((1,), (1,)), ((), ())), preferred_element_type=F32) * decay
        inner = jnp.dot(sc.astype(BF16), v, preferred_element_type=F32)
        cross = jnp.dot((qf * qdec_f).astype(BF16), state.astype(BF16), preferred_element_type=F32)
        acc_ref[r, :] = inner + cross
        kv = lax.dot_general((kf * kdec_f).astype(BF16), v, (((0,), (0,)), ((), ())),
                             preferred_element_type=F32)
        return state * cdec_f + kv

    lax.fori_loop(0, n_chunks, fwd, jnp.zeros((dk, dv), F32))

    def bwd(i, state):
        c = n_chunks - 1 - i
        q, k, v, r = load(c)
        qf, kf = q.astype(F32), k.astype(F32)
        cross = jnp.dot((qf * qdec_b).astype(BF16), state.astype(BF16), preferred_element_type=F32)
        acc_ref[r, :] = acc_ref[r, :] + cross
        kv = lax.dot_general((kf * kdec_b).astype(BF16), v, (((0,), (0,)), ((), ())),
                             preferred_element_type=F32)
        return state * cdec_b + kv

    lax.fori_loop(0, n_chunks, bwd, jnp.zeros((dk, dv), F32))

    ro = acc_ref[...]
    mu = jnp.mean(ro, axis=-1, keepdims=True)
    xc = ro - mu
    y = xc * lax.rsqrt(jnp.mean(xc * xc, axis=-1, keepdims=True) + NORM_EPS) * gain_ref[...]
    o_ref[0] = (_silu(gate_ref[0].astype(F32)) * y).astype(BF16)


def _retention(proj3, decay_logit, norm_gain):
    b, s, _ = proj3.shape
    h = RET_HEADS
    base = 3 * DIFF_HEADS
    kern = functools.partial(_retention_kernel, chunk=min(RET_CHUNK, s))

    def col(off):
        return pl.BlockSpec((1, s, LANE), lambda bi, hi: (bi, 0, base + off * h + hi))

    return pl.pallas_call(
        kern,
        grid=(b, h),
        in_specs=[pl.BlockSpec((2, h), lambda bi, hi: (0, 0)), col(0), col(1), col(2), col(3),
                  pl.BlockSpec((1, LANE), lambda bi, hi: (0, hi))],
        out_specs=pl.BlockSpec((1, s, LANE), lambda bi, hi: (bi, 0, hi)),
        out_shape=jax.ShapeDtypeStruct((b, s, h * LANE), BF16),
        scratch_shapes=[pltpu.VMEM((s, LANE), F32)],
        compiler_params=_cparams(("arbitrary", "arbitrary")),
        name="retention",
    )(decay_logit, proj3, proj3, proj3, proj3, norm_gain)


def _outproj_kernel(ad_ref, ro_ref, wa_ref, wr_ref, x_ref, g1_ref, sc2_ref, sh2_ref, gpost_ref, gpre_ref,
                    wrt_ref, x1_ref, h2_ref, lt_ref):
    mix = (jnp.dot(ad_ref[...], wa_ref[...], preferred_element_type=F32)
           + jnp.dot(ro_ref[...], wr_ref[...], preferred_element_type=F32))
    x1 = x_ref[...] + g1_ref[0] * (_rms(mix) * gpost_ref[...])
    x1_ref[...] = x1
    h2 = _rms(x1) * gpre_ref[...] * (1.0 + sc2_ref[0]) + sh2_ref[0]
    h2_ref[...] = h2
    lt_ref[...] = lax.dot_general(wrt_ref[...], h2, (((1,), (1,)), ((), ())), precision=HIGHEST,
                                  preferred_element_type=F32)


def _outproj(ad2, ro2, wa_bf, wr_bf, x2, gate1, scale2, shift2, gpost, gpre, w_router_t, seq):
    t, d = x2.shape
    ka, kr = ad2.shape[1], ro2.shape[1]
    e = w_router_t.shape[0]
    tm = min(seq, 512)
    tiles_per_seq = seq // tm
    mod_spec = pl.BlockSpec((1, 1, d), lambda i: (i // tiles_per_seq, 0, 0))
    vec_spec = pl.BlockSpec((1, d), lambda i: (0, 0))
    row_spec = pl.BlockSpec((tm, d), lambda i: (i, 0))
    return pl.pallas_call(
        _outproj_kernel,
        grid=(t // tm,),
        in_specs=[pl.BlockSpec((tm, ka), lambda i: (i, 0)), pl.BlockSpec((tm, kr), lambda i: (i, 0)),
                  pl.BlockSpec((ka, d), lambda i: (0, 0)), pl.BlockSpec((kr, d), lambda i: (0, 0)),
                  row_spec, mod_spec, mod_spec, mod_spec, vec_spec, vec_spec,
                  pl.BlockSpec((e, d), lambda i: (0, 0))],
        out_specs=[row_spec, row_spec, pl.BlockSpec((e, tm), lambda i: (0, i))],
        out_shape=[jax.ShapeDtypeStruct((t, d), F32), jax.ShapeDtypeStruct((t, d), F32),
                   jax.ShapeDtypeStruct((e, t), F32)],
        compiler_params=_cparams(("arbitrary",)),
        name="outproj",
    )(ad2, ro2, wa_bf, wr_bf, x2, gate1, scale2, shift2, gpost, gpre, w_router_t)


def _lane_cumsum_exclusive(x, tri):
    r, s = x.shape
    carry = jnp.zeros((r, 1), F32)
    cols = []
    for blk in range(s // LANE):
        xb = x[:, blk * LANE:(blk + 1) * LANE]
        inc = jnp.dot(xb.astype(BF16), tri, preferred_element_type=F32)
        cols.append(inc - xb + carry)
        carry = carry + inc[:, LANE - 1:LANE]
    return jnp.concatenate(cols, axis=1)


def _route_kernel(lt_ref, post_ref, info_ref, *, cap):
    logits = lt_ref[...]
    n_e, s_len = logits.shape
    mx = jnp.max(logits, axis=0, keepdims=True)
    ex = jnp.exp(logits - mx)
    aff = ex / jnp.sum(ex, axis=0, keepdims=True)

    kf = float(cap)
    tbits = jnp.zeros((n_e, 1), jnp.int32)
    for bit in range(30, -1, -1):
        cand = tbits | (1 << bit)
        cnt = jnp.sum(jnp.where(aff >= pltpu.bitcast(cand, F32), 1.0, 0.0), axis=1, keepdims=True)
        tbits = jnp.where(cnt >= kf, cand, tbits)
    thr = pltpu.bitcast(tbits, F32)

    ri = lax.broadcasted_iota(jnp.int32, (LANE, LANE), 0)
    rj = lax.broadcasted_iota(jnp.int32, (LANE, LANE), 1)
    tri = jnp.where(ri <= rj, 1.0, 0.0).astype(BF16)
    gt = jnp.where(aff > thr, 1.0, 0.0)
    tie = jnp.where(aff == thr, 1.0, 0.0)
    need = kf - jnp.sum(gt, axis=1, keepdims=True)
    tie_rank = _lane_cumsum_exclusive(tie, tri)
    sel = gt + tie * jnp.where(tie_rank < need, 1.0, 0.0)
    pos = _lane_cumsum_exclusive(sel, tri)
    pos = jnp.where(sel > 0.5, pos, -1.0)

    stacked = jnp.concatenate([pos, aff, jnp.zeros((LANE - 2 * n_e, s_len), F32)], axis=0)
    tok_major = stacked.T
    a1 = tok_major.astype(BF16).astype(F32)
    r1 = tok_major - a1
    a2 = r1.astype(BF16).astype(F32)
    a3 = (r1 - a2).astype(BF16).astype(F32)
    tok = lax.broadcasted_iota(jnp.int32, (s_len, LANE), 0)
    lane = lax.broadcasted_iota(jnp.int32, (s_len, LANE), 1)
    in_a = (lane >= n_e) & (lane < 2 * n_e)
    cols = jnp.where(lane == 0, (tok // 64).astype(F32), jnp.where(lane == 1, (tok % 64).astype(F32), 0.0))
    cols = jnp.where(in_a, a1, cols)
    cols = cols + pltpu.roll(jnp.where(in_a, a2, 0.0), n_e, 1) + pltpu.roll(jnp.where(in_a, a3, 0.0), 2 * n_e, 1)
    cols = cols.astype(BF16)
    slot = lax.broadcasted_iota(jnp.int32, (cap, s_len), 0).astype(F32)
    post_ref[0] = tok_major.astype(jnp.int32)
    for e in range(n_e):
        onehot_t = jnp.where(pos[e:e + 1, :] == slot, 1.0, 0.0).astype(BF16)
        res = jnp.dot(onehot_t, cols, preferred_element_type=F32).T
        tok_idx = res[0:1, :] * 64.0 + res[1:2, :]
        gate = (res[n_e + e:n_e + e + 1, :] + res[2 * n_e + e:2 * n_e + e + 1, :]
                + res[3 * n_e + e:3 * n_e + e + 1, :])
        info_ref[0, e] = jnp.concatenate([tok_idx, gate, jnp.zeros((6, cap), F32)], axis=0)


def _route(logits_t, b, s, cap):
    n_e = logits_t.shape[0]
    kern = functools.partial(_route_kernel, cap=cap)
    return pl.pallas_call(
        kern,
        grid=(b,),
        in_specs=[pl.BlockSpec((n_e, s), lambda bi: (0, bi))],
        out_specs=[pl.BlockSpec((1, s, LANE), lambda bi: (bi, 0, 0)),
                   pl.BlockSpec((1, n_e, 8, cap), lambda bi: (bi, 0, 0, 0))],
        out_shape=[jax.ShapeDtypeStruct((b, s, LANE), jnp.int32),
                   jax.ShapeDtypeStruct((b, n_e, 8, cap), F32)],
        compiler_params=_cparams(("arbitrary",)),
        name="route",
    )(logits_t)


def _gather_kernel(rows_ref, src_ref, dst_ref, sem, *, rows_per_step):
    base = pl.program_id(0) * rows_per_step

    def copy(k):
        return pltpu.make_async_copy(src_ref.at[pl.ds(rows_ref[base + k], 1)],
                                     dst_ref.at[pl.ds(base + k, 1)], sem)

    def start(k, carry):
        copy(k).start()
        return carry

    def wait(k, carry):
        copy(k).wait()
        return carry

    lax.fori_loop(0, rows_per_step, start, 0)
    lax.fori_loop(0, rows_per_step, wait, 0)


def _gather_rows(rows, src):
    n = rows.shape[0]
    d = src.shape[1]
    rows_per_step = min(n, 1024)
    kern = functools.partial(_gather_kernel, rows_per_step=rows_per_step)
    return pl.pallas_call(
        kern,
        grid_spec=pltpu.PrefetchScalarGridSpec(
            num_scalar_prefetch=1,
            grid=(n // rows_per_step,),
            in_specs=[pl.BlockSpec(memory_space=pl.ANY)],
            out_specs=pl.BlockSpec(memory_space=pl.ANY),
            scratch_shapes=[pltpu.SemaphoreType.DMA(())]),
        out_shape=jax.ShapeDtypeStruct((n, d), src.dtype),
        compiler_params=_cparams(("arbitrary",)),
        name="gather",
    )(rows, src)


def _ffn_kernel(x_ref, gate_ref, wg_ref, wu_ref, wd_ref, y_ref, xb_ref, acc_ref):
    f = pl.program_id(2)

    @pl.when(f == 0)
    def _():
        xb_ref[...] = x_ref[...].astype(BF16)
        acc_ref[...] = jnp.zeros_like(acc_ref)

    xb = xb_ref[...]
    a = jnp.dot(xb, wg_ref[0].astype(BF16), preferred_element_type=F32)
    u = jnp.dot(xb, wu_ref[0].astype(BF16), preferred_element_type=F32)
    hmid = (_silu(a) * u).astype(BF16)
    acc_ref[...] += jnp.dot(hmid, wd_ref[0].astype(BF16), preferred_element_type=F32)

    @pl.when(f == pl.num_programs(2) - 1)
    def _():
        y_ref[...] = (acc_ref[...] * gate_ref[...]).astype(BF16)


def _ffn(xg, gate_col, w_gate, w_up, w_down, rows_per_expert):
    n, d = xg.shape
    n_e, _, ff = w_gate.shape
    tm = min(rows_per_expert, 1024)
    tf = min(ff, 256)
    tiles_per_expert = rows_per_expert // tm
    row_map = lambda e, r, f: (e * tiles_per_expert + r, 0)
    return pl.pallas_call(
        _ffn_kernel,
        grid=(n_e, tiles_per_expert, ff // tf),
        in_specs=[pl.BlockSpec((tm, d), row_map), pl.BlockSpec((tm, 1), row_map),
                  pl.BlockSpec((1, d, tf), lambda e, r, f: (e, 0, f)),
                  pl.BlockSpec((1, d, tf), lambda e, r, f: (e, 0, f)),
                  pl.BlockSpec((1, tf, d), lambda e, r, f: (e, f, 0))],
        out_specs=pl.BlockSpec((tm, d), row_map),
        out_shape=jax.ShapeDtypeStruct((n, d), BF16),
        scratch_shapes=[pltpu.VMEM((tm, d), BF16), pltpu.VMEM((tm, d), F32)],
        compiler_params=_cparams(("arbitrary", "arbitrary", "arbitrary")),
        name="ffn",
    )(xg, gate_col, w_gate, w_up, w_down)


def _combine_kernel(post_ref, y_ref, x1_ref, g2_ref, gpost_ref, o_ref, acc_ref):
    e = pl.program_id(2)

    @pl.when(e == 0)
    def _():
        acc_ref[...] = jnp.zeros_like(acc_ref)

    posmat = post_ref[0]
    lane = lax.broadcasted_iota(jnp.int32, posmat.shape, 1)
    pos = jnp.sum(jnp.where(lane == e, posmat, 0), axis=1, keepdims=True)
    cap = y_ref.shape[0]
    slot = lax.broadcasted_iota(jnp.int32, (pos.shape[0], cap), 1)
    onehot = jnp.where(pos == slot, 1.0, 0.0).astype(BF16)
    acc_ref[...] += jnp.dot(onehot, y_ref[...], preferred_element_type=F32)

    @pl.when(e == pl.num_programs(2) - 1)
    def _():
        o_ref[...] = x1_ref[...] + g2_ref[0] * (_rms(acc_ref[...]) * gpost_ref[...])


def _combine(pos_t, y, x1, gate2, gpost, b, s, cap):
    t, d = x1.shape
    n_e = y.shape[0] // (b * cap)
    tt = min(s, 1024)
    tiles_per_seq = s // tt
    return pl.pallas_call(
        _combine_kernel,
        grid=(b, tiles_per_seq, n_e),
        in_specs=[pl.BlockSpec((1, tt, LANE), lambda bi, ji, e: (bi, ji, 0)),
                  pl.BlockSpec((cap, d), lambda bi, ji, e: (e * b + bi, 0)),
                  pl.BlockSpec((tt, d), lambda bi, ji, e: (bi * tiles_per_seq + ji, 0)),
                  pl.BlockSpec((1, 1, d), lambda bi, ji, e: (bi, 0, 0)),
                  pl.BlockSpec((1, d), lambda bi, ji, e: (0, 0))],
        out_specs=pl.BlockSpec((tt, d), lambda bi, ji, e: (bi * tiles_per_seq + ji, 0)),
        out_shape=jax.ShapeDtypeStruct((t, d), F32),
        scratch_shapes=[pltpu.VMEM((tt, d), F32)],
        compiler_params=_cparams(("arbitrary", "arbitrary", "arbitrary")),
        name="combine",
    )(pos_t, y, x1, gate2, gpost)


def kernel(x, c, positions, w_ada, b_ada, g_pre_mix, g_post_mix, w_in, diff_lambda_q1, diff_lambda_k1,
           diff_lambda_q2, diff_lambda_k2, diff_subln_gain, ret_decay_logit, ret_norm_gain, w_out, g_pre_ffn,
           g_post_ffn, w_router, w_gate, w_up, w_down):
    b, s, d = x.shape
    t = b * s
    depth = w_ada.shape[0]
    n_e = w_router.shape[2]
    cap = EC_CAPACITY_FACTOR * s // n_e
    diff_w = DIFF_HEADS * LANE

    pos_col = positions.reshape(t, 1)
    tabs = _rope_tables(pos_col)
    c_pad = jnp.pad(c, ((0, (-b) % 8), (0, 0)))
    x2 = x.reshape(t, d)
    for l in range(depth):
        lam_init = 0.8 - 0.6 * math.exp(-0.3 * l)
        mod = _ada(c_pad, w_ada[l], b_ada[l].reshape(1, -1))[:b]
        shift1, scale1, gate1, shift2, scale2, gate2 = [m.reshape(b, 1, d) for m in jnp.split(mod, 6, axis=-1)]

        proj = _inproj(x2, scale1, shift1, g_pre_mix[l].reshape(1, d), w_in[l].astype(BF16), tabs, s)
        proj3 = proj.reshape(b, s, -1)
        lams = [v[l].reshape(1, -1) for v in (diff_lambda_q1, diff_lambda_k1, diff_lambda_q2, diff_lambda_k2)]
        ad = _diffattn(proj3, lams, diff_subln_gain[l].reshape(1, -1), lam_init)
        ro = _retention(proj3, ret_decay_logit[l], ret_norm_gain[l].reshape(1, -1))

        w_out_bf = w_out[l].astype(BF16)
        x1, h2, logits_t = _outproj(
            ad.reshape(t, -1), ro.reshape(t, -1), w_out_bf[:diff_w], w_out_bf[diff_w:], x2, gate1, scale2,
            shift2, g_post_mix[l].reshape(1, d), g_pre_ffn[l].reshape(1, d), w_router[l].T, s)

        pos_t, info = _route(logits_t, b, s, cap)
        rows = info[:, :, 0, :].astype(jnp.int32) + (jnp.arange(b, dtype=jnp.int32) * s)[:, None, None]
        rows = jnp.transpose(rows, (1, 0, 2)).reshape(-1)
        gate_col = jnp.transpose(info[:, :, 1, :], (1, 0, 2)).reshape(-1, 1)
        xg = _gather_rows(rows, h2)
        y = _ffn(xg, gate_col, w_gate[l], w_up[l], w_down[l], b * cap)
        x2 = _combine(pos_t, y, x1, gate2, g_post_ffn[l].reshape(1, d), b, s, cap)
    return x2.reshape(b, s, d)
```

```python
import functools
import math

import jax
import jax.numpy as jnp
from jax import lax
from jax.experimental import pallas as pl
from jax.experimental.pallas import tpu as pltpu

F32 = jnp.float32
BF16 = jnp.bfloat16
HIGHEST = lax.Precision.HIGHEST

DIFF_HEADS = 8
DIFF_HEAD_DIM = 64
RET_HEADS = 8
RET_QK_DIM = 128
LANE = 128
N_EXPERTS = 16
EC_CAPACITY_FACTOR = 2
ROPE_THETA = 10000.0
NORM_EPS = 1e-6
RET_CHUNK = 128
VMEM_LIMIT = 56 * 1024 * 1024


def _cparams(sem):
    return pltpu.CompilerParams(dimension_semantics=sem, vmem_limit_bytes=VMEM_LIMIT)


def _silu(x):
    return x / (1.0 + jnp.exp(-x))


def _rms(x, eps=NORM_EPS):
    return x * lax.rsqrt(jnp.mean(x * x, axis=-1, keepdims=True) + eps)


def _ada_kernel(c_ref, w_ref, b_ref, o_ref):
    s = _silu(c_ref[...])
    o_ref[...] = jnp.dot(s, w_ref[...], precision=HIGHEST, preferred_element_type=F32) + b_ref[...]


def _ada(c_pad, w_ada, b_ada):
    m, d = c_pad.shape
    n = w_ada.shape[1]
    tn = min(d, 1024)
    assert n % tn == 0
    return pl.pallas_call(
        _ada_kernel,
        grid=(n // tn,),
        in_specs=[pl.BlockSpec((m, d), lambda j: (0, 0)),
                  pl.BlockSpec((d, tn), lambda j: (0, j)),
                  pl.BlockSpec((1, tn), lambda j: (0, j))],
        out_specs=pl.BlockSpec((m, tn), lambda j: (0, j)),
        out_shape=jax.ShapeDtypeStruct((m, n), F32),
        compiler_params=_cparams(("arbitrary",)),
        name="ada",
    )(c_pad, w_ada, b_ada)


def _rope_tab_kernel(pos_ref, cd_ref, sda_ref, sdb_ref, cr_ref, sr_ref):
    pos = pos_ref[...].astype(F32)
    lane = lax.broadcasted_iota(jnp.int32, (pos.shape[0], LANE), 1)
    is_d = lane < 32
    expo = jnp.where(is_d, lane.astype(F32) * (2.0 / DIFF_HEAD_DIM),
                     (lane - 32).astype(F32) * (2.0 / RET_QK_DIM))
    inv = jnp.exp(expo * (-math.log(ROPE_THETA)))
    ang = pos * inv
    c = jnp.cos(ang)
    s = jnp.sin(ang)
    c32, c64, c96 = pltpu.roll(c, 32, 1), pltpu.roll(c, 64, 1), pltpu.roll(c, 96, 1)
    s32, s64, s96 = pltpu.roll(s, 32, 1), pltpu.roll(s, 64, 1), pltpu.roll(s, 96, 1)
    q = lane // 32
    cos_d = jnp.where(q == 0, c, jnp.where(q == 1, c32, jnp.where(q == 2, c64, c96)))
    sin_d = jnp.where(q == 0, s, jnp.where(q == 1, s32, jnp.where(q == 2, s64, s96)))
    first_half = (q == 0) | (q == 2)
    cd_ref[...] = cos_d
    sda_ref[...] = jnp.where(first_half, -sin_d, 0.0)
    sdb_ref[...] = jnp.where(first_half, 0.0, sin_d)
    lo = lane < 64
    cr_ref[...] = jnp.where(lo, c96, c32)
    sr_ref[...] = jnp.where(lo, -s96, s32)


def _rope_tables(pos_col):
    t = pos_col.shape[0]
    tm = min(t, 2048)
    spec = pl.BlockSpec((tm, LANE), lambda i: (i, 0))
    return pl.pallas_call(
        _rope_tab_kernel,
        grid=(t // tm,),
        in_specs=[pl.BlockSpec((tm, 1), lambda i: (i, 0))],
        out_specs=[spec] * 5,
        out_shape=[jax.ShapeDtypeStruct((t, LANE), F32)] * 5,
        compiler_params=_cparams(("arbitrary",)),
        name="rope_tab",
    )(pos_col)


def _inproj_kernel(x_ref, sc_ref, sh_ref, g_ref, w_ref, cd_ref, sda_ref, sdb_ref, cr_ref, sr_ref,
                   o_ref, h_ref, *, blocks_per_type, q_scale, rk_scale):
    j = pl.program_id(1)

    @pl.when(j == 0)
    def _():
        h = _rms(x_ref[...]) * g_ref[...] * (1.0 + sc_ref[0]) + sh_ref[0]
        h_ref[...] = h.astype(BF16)

    acc = jnp.dot(h_ref[...], w_ref[...], preferred_element_type=F32)
    ctype = j // blocks_per_type
    nslab = acc.shape[1] // LANE

    def rope_d(scale):
        cd, sa, sb = cd_ref[...], sda_ref[...], sdb_ref[...]
        for s in range(nslab):
            xs = acc[:, s * LANE:(s + 1) * LANE]
            y = xs * cd + pltpu.roll(xs, 96, 1) * sa + pltpu.roll(xs, 32, 1) * sb
            o_ref[:, s * LANE:(s + 1) * LANE] = (y * scale).astype(BF16)

    def rope_r(scale):
        cr, sr = cr_ref[...], sr_ref[...]
        for s in range(nslab):
            xs = acc[:, s * LANE:(s + 1) * LANE]
            y = xs * cr + pltpu.roll(xs, 64, 1) * sr
            o_ref[:, s * LANE:(s + 1) * LANE] = (y * scale).astype(BF16)

    @pl.when(ctype == 0)
    def _():
        rope_d(q_scale)

    @pl.when(ctype == 1)
    def _():
        rope_d(1.0)

    @pl.when(ctype == 3)
    def _():
        rope_r(1.0)

    @pl.when(ctype == 4)
    def _():
        rope_r(rk_scale)

    @pl.when((ctype == 2) | (ctype >= 5))
    def _():
        o_ref[...] = acc.astype(BF16)


def _inproj(x2, scale1, shift1, g, w_bf, tabs, seq):
    t, d = x2.shape
    nc = w_bf.shape[1]
    type_w = nc // 7
    tm = min(seq, 1024)
    tn = min(type_w, 1024)
    tiles_per_seq = seq // tm
    kern = functools.partial(
        _inproj_kernel, blocks_per_type=type_w // tn,
        q_scale=(DIFF_HEAD_DIM ** -0.5) * math.log2(math.e), rk_scale=RET_QK_DIM ** -0.5)
    mod_spec = pl.BlockSpec((1, 1, d), lambda i, j: (i // tiles_per_seq, 0, 0))
    tab_spec = pl.BlockSpec((tm, LANE), lambda i, j: (i, 0))
    return pl.pallas_call(
        kern,
        grid=(t // tm, nc // tn),
        in_specs=[pl.BlockSpec((tm, d), lambda i, j: (i, 0)), mod_spec, mod_spec,
                  pl.BlockSpec((1, d), lambda i, j: (0, 0)),
                  pl.BlockSpec((d, tn), lambda i, j: (0, j))] + [tab_spec] * 5,
        out_specs=pl.BlockSpec((tm, tn), lambda i, j: (i, j)),
        out_shape=jax.ShapeDtypeStruct((t, nc), BF16),
        scratch_shapes=[pltpu.VMEM((tm, d), BF16)],
        compiler_params=_cparams(("arbitrary", "arbitrary")),
        name="inproj",
    )(x2, scale1, shift1, g, w_bf, *tabs)


def _diffattn_kernel(lq1_ref, lk1_ref, lq2_ref, lk2_ref, q_ref, k_ref, v_ref, g_ref, o_ref,
                     vt_ref, s_ref, *, lam_init, tk):
    @pl.when(pl.program_id(2) == 0)
    def _():
        dv = v_ref.shape[2]
        extra = lax.broadcasted_iota(jnp.int32, (vt_ref.shape[0] - dv, v_ref.shape[1]), 0)
        vt_ref[0:dv, :] = v_ref[0].astype(F32).T.astype(BF16)
        vt_ref[dv:, :] = jnp.where(extra == 0, 1.0, 0.0).astype(BF16)

    lam = (jnp.exp(jnp.sum(lq1_ref[...] * lk1_ref[...], axis=-1, keepdims=True))
           - jnp.exp(jnp.sum(lq2_ref[...] * lk2_ref[...], axis=-1, keepdims=True)) + lam_init)

    q = q_ref[0]
    tq = q.shape[0]
    s_len = k_ref.shape[1]
    nk = s_len // tk
    lane = lax.broadcasted_iota(jnp.int32, q.shape, 1)
    dv = v_ref.shape[2]
    outs = []
    for m in range(2):
        in_map = (lane >= m * DIFF_HEAD_DIM) & (lane < (m + 1) * DIFF_HEAD_DIM)
        qm = jnp.where(in_map, q, jnp.zeros_like(q))
        cmax = jnp.full((8, tq), -jnp.inf, F32)
        for c in range(nk):
            kc = k_ref[0, c * tk:(c + 1) * tk, :]
            s = lax.dot_general(kc, qm, (((1,), (1,)), ((), ())), preferred_element_type=F32)
            s_ref[m, c * tk:(c + 1) * tk, :] = s
            cmax = jnp.maximum(cmax, jnp.max(s.reshape(tk // 8, 8, tq), axis=0))
        mx = jnp.max(cmax, axis=0, keepdims=True)
        acc = jnp.zeros((vt_ref.shape[0], tq), F32)
        for c in range(nk):
            p = jnp.exp2((s_ref[m, c * tk:(c + 1) * tk, :] - mx).astype(BF16))
            acc = acc + jnp.dot(vt_ref[:, c * tk:(c + 1) * tk], p, preferred_element_type=F32)
        outs.append(acc[0:dv, :] / acc[dv:dv + 1, :])
    ad = outs[0] - lam * outs[1]
    y = ad * lax.rsqrt(jnp.mean(ad * ad, axis=0, keepdims=True) + NORM_EPS)
    o_ref[0] = (y.T * (g_ref[...] * (1.0 - lam_init))).astype(BF16)


def _diffattn(proj3, lams, subln_gain, lam_init):
    b, s, _ = proj3.shape
    h = DIFF_HEADS
    tq = min(s, 512)
    tk = min(s, 512)
    kern = functools.partial(_diffattn_kernel, lam_init=lam_init, tk=tk)
    lam_spec = pl.BlockSpec((1, DIFF_HEAD_DIM), lambda bi, hi, qi: (0, 0))
    return pl.pallas_call(
        kern,
        grid=(b, h, s // tq),
        in_specs=[lam_spec] * 4 + [
            pl.BlockSpec((1, tq, LANE), lambda bi, hi, qi: (bi, qi, hi)),
            pl.BlockSpec((1, s, LANE), lambda bi, hi, qi: (bi, 0, h + hi)),
            pl.BlockSpec((1, s, LANE), lambda bi, hi, qi: (bi, 0, 2 * h + hi)),
            pl.BlockSpec((1, LANE), lambda bi, hi, qi: (0, 0))],
        out_specs=pl.BlockSpec((1, tq, LANE), lambda bi, hi, qi: (bi, qi, hi)),
        out_shape=jax.ShapeDtypeStruct((b, s, h * LANE), BF16),
        scratch_shapes=[pltpu.VMEM((LANE + 16, s), BF16), pltpu.VMEM((2, s, tq), F32)],
        compiler_params=_cparams(("arbitrary", "arbitrary", "arbitrary")),
        name="diffattn",
    )(*lams, proj3, proj3, proj3, subln_gain)


def _retention_kernel(logit_ref, q_ref, k_ref, v_ref, gate_ref, gain_ref, o_ref, acc_ref, *, chunk):
    hi = pl.program_id(1)
    c_len = chunk
    s_len = q_ref.shape[1]
    n_chunks = s_len // c_len
    lane = lax.broadcasted_iota(jnp.int32, (1, logit_ref.shape[1]), 1)
    lg = logit_ref[...]
    log_g = -jnp.log(1.0 + jnp.exp(-lg))
    log_g = jnp.sum(jnp.where(lane == hi, log_g, 0.0), axis=-1, keepdims=True)
    lgf, lgb = log_g[0:1, :], log_g[1:2, :]

    ii = lax.broadcasted_iota(jnp.int32, (c_len, c_len), 0).astype(F32)
    jj = lax.broadcasted_iota(jnp.int32, (c_len, c_len), 1).astype(F32)
    dist = ii - jj
    decay = jnp.where(dist >= 0, jnp.exp(lgf * jnp.maximum(dist, 0.0)),
                      jnp.exp(lgb * jnp.maximum(-dist, 0.0)))
    ci = lax.broadcasted_iota(jnp.int32, (c_len, 1), 0).astype(F32)
    qdec_f = jnp.exp(lgf * (ci + 1.0))
    kdec_f = jnp.exp(lgf * (c_len - 1.0 - ci))
    qdec_b = jnp.exp(lgb * (c_len - ci))
    kdec_b = jnp.exp(lgb * ci)
    cdec_f = jnp.exp(lgf * c_len)
    cdec_b = jnp.exp(lgb * c_len)
    dk = q_ref.shape[2]
    dv = v_ref.shape[2]

    def load(c):
        r = pl.ds(pl.multiple_of(c * c_len, c_len), c_len)
        return q_ref[0, r, :], k_ref[0, r, :], v_ref[0, r, :], r

    def fwd(c, state):
        q, k, v, r = load(c)
        qf, kf = q.astype(F32), k.astype(F32)
        sc = lax.dot_general(q, k, (((1,), (1,)), ((), ())), preferred_element_type=F32) * decay
        inner = jnp.dot(sc.astype(BF16), v, preferred_element_type=F32)
        cross = jnp.dot((qf * qdec_f).astype(BF16), state.astype(BF16), preferred_element_type=F32)
        acc_ref[r, :] = inner + cross
        kv = lax.dot_general((kf * kdec_f).astype(BF16), v, (((0,), (0,)), ((), ())),
                             preferred_element_type=F32)
        return state * cdec_f + kv

    lax.fori_loop(0, n_chunks, fwd, jnp.zeros((dk, dv), F32))

    def bwd(i, state):
        c = n_chunks - 1 - i
        q, k, v, r = load(c)
        qf, kf = q.astype(F32), k.astype(F32)
        cross = jnp.dot((qf * qdec_b).astype(BF16), state.astype(BF16), preferred_element_type=F32)
        acc_ref[r, :] = acc_ref[r, :] + cross
        kv = lax.dot_general((kf * kdec_b).astype(BF16), v, (((0,), (0,)), ((), ())),
                             preferred_element_type=F32)
        return state * cdec_b + kv

    lax.fori_loop(0, n_chunks, bwd, jnp.zeros((dk, dv), F32))

    ro = acc_ref[...]
    mu = jnp.mean(ro, axis=-1, keepdims=True)
    xc = ro - mu
    y = xc * lax.rsqrt(jnp.mean(xc * xc, axis=-1, keepdims=True) + NORM_EPS) * gain_ref[...]
    o_ref[0] = (_silu(gate_ref[0].astype(F32)) * y).astype(BF16)


def _retention(proj3, decay_logit, norm_gain):
    b, s, _ = proj3.shape
    h = RET_HEADS
    base = 3 * DIFF_HEADS
    kern = functools.partial(_retention_kernel, chunk=min(RET_CHUNK, s))

    def col(off):
        return pl.BlockSpec((1, s, LANE), lambda bi, hi: (bi, 0, base + off * h + hi))

    return pl.pallas_call(
        kern,
        grid=(b, h),
        in_specs=[pl.BlockSpec((2, h), lambda bi, hi: (0, 0)), col(0), col(1), col(2), col(3),
                  pl.BlockSpec((1, LANE), lambda bi, hi: (0, hi))],
        out_specs=pl.BlockSpec((1, s, LANE), lambda bi, hi: (bi, 0, hi)),
        out_shape=jax.ShapeDtypeStruct((b, s, h * LANE), BF16),
        scratch_shapes=[pltpu.VMEM((s, LANE), F32)],
        compiler_params=_cparams(("arbitrary", "arbitrary")),
        name="retention",
    )(decay_logit, proj3, proj3, proj3, proj3, norm_gain)


def _outproj_kernel(ad_ref, ro_ref, wa_ref, wr_ref, x_ref, g1_ref, sc2_ref, sh2_ref, gpost_ref, gpre_ref,
                    wrt_ref, x1_ref, h2_ref, lt_ref):
    mix = (jnp.dot(ad_ref[...], wa_ref[...], preferred_element_type=F32)
           + jnp.dot(ro_ref[...], wr_ref[...], preferred_element_type=F32))
    x1 = x_ref[...] + g1_ref[0] * (_rms(mix) * gpost_ref[...])
    x1_ref[...] = x1
    h2 = _rms(x1) * gpre_ref[...] * (1.0 + sc2_ref[0]) + sh2_ref[0]
    h2_ref[...] = h2
    lt_ref[...] = lax.dot_general(wrt_ref[...], h2, (((1,), (1,)), ((), ())), precision=HIGHEST,
                                  preferred_element_type=F32)


def _outproj(ad2, ro2, wa_bf, wr_bf, x2, gate1, scale2, shift2, gpost, gpre, w_router_t, seq):
    t, d = x2.shape
    ka, kr = ad2.shape[1], ro2.shape[1]
    e = w_router_t.shape[0]
    tm = min(seq, 512)
    tiles_per_seq = seq // tm
    mod_spec = pl.BlockSpec((1, 1, d), lambda i: (i // tiles_per_seq, 0, 0))
    vec_spec = pl.BlockSpec((1, d), lambda i: (0, 0))
    row_spec = pl.BlockSpec((tm, d), lambda i: (i, 0))
    return pl.pallas_call(
        _outproj_kernel,
        grid=(t // tm,),
        in_specs=[pl.BlockSpec((tm, ka), lambda i: (i, 0)), pl.BlockSpec((tm, kr), lambda i: (i, 0)),
                  pl.BlockSpec((ka, d), lambda i: (0, 0)), pl.BlockSpec((kr, d), lambda i: (0, 0)),
                  row_spec, mod_spec, mod_spec, mod_spec, vec_spec, vec_spec,
                  pl.BlockSpec((e, d), lambda i: (0, 0))],
        out_specs=[row_spec, row_spec, pl.BlockSpec((e, tm), lambda i: (0, i))],
        out_shape=[jax.ShapeDtypeStruct((t, d), F32), jax.ShapeDtypeStruct((t, d), F32),
                   jax.ShapeDtypeStruct((e, t), F32)],
        compiler_params=_cparams(("arbitrary",)),
        name="outproj",
    )(ad2, ro2, wa_bf, wr_bf, x2, gate1, scale2, shift2, gpost, gpre, w_router_t)


def _lane_cumsum_exclusive(x, tri):
    r, s = x.shape
    carry = jnp.zeros((r, 1), F32)
    cols = []
    for blk in range(s // LANE):
        xb = x[:, blk * LANE:(blk + 1) * LANE]
        inc = jnp.dot(xb.astype(BF16), tri, preferred_element_type=F32)
        cols.append(inc - xb + carry)
        carry = carry + inc[:, LANE - 1:LANE]
    return jnp.concatenate(cols, axis=1)


def _route_kernel(lt_ref, post_ref, info_ref, *, cap):
    logits = lt_ref[...]
    n_e, s_len = logits.shape
    mx = jnp.max(logits, axis=0, keepdims=True)
    ex = jnp.exp(logits - mx)
    aff = ex / jnp.sum(ex, axis=0, keepdims=True)

    kf = float(cap)
    tbits = jnp.zeros((n_e, 1), jnp.int32)
    for bit in range(30, -1, -1):
        cand = tbits | (1 << bit)
        cnt = jnp.sum(jnp.where(aff >= pltpu.bitcast(cand, F32), 1.0, 0.0), axis=1, keepdims=True)
        tbits = jnp.where(cnt >= kf, cand, tbits)
    thr = pltpu.bitcast(tbits, F32)

    ri = lax.broadcasted_iota(jnp.int32, (LANE, LANE), 0)
    rj = lax.broadcasted_iota(jnp.int32, (LANE, LANE), 1)
    tri = jnp.where(ri <= rj, 1.0, 0.0).astype(BF16)
    gt = jnp.where(aff > thr, 1.0, 0.0)
    tie = jnp.where(aff == thr, 1.0, 0.0)
    need = kf - jnp.sum(gt, axis=1, keepdims=True)
    tie_rank = _lane_cumsum_exclusive(tie, tri)
    sel = gt + tie * jnp.where(tie_rank < need, 1.0, 0.0)
    pos = _lane_cumsum_exclusive(sel, tri)
    pos = jnp.where(sel > 0.5, pos, -1.0)

    tok_major = jnp.concatenate([pos, jnp.zeros((LANE - n_e, s_len), F32)], axis=0).T
    post_ref[0] = tok_major.astype(jnp.int32)
    a1 = aff.astype(BF16)
    r1 = aff - a1.astype(F32)
    a2 = r1.astype(BF16)
    a3 = (r1 - a2.astype(F32)).astype(BF16)
    tok = lax.broadcasted_iota(jnp.int32, (n_e, s_len), 1)
    row = lax.broadcasted_iota(jnp.int32, (n_e, s_len), 0)
    digits = jnp.where(row == 0, tok // 64, jnp.where(row == 1, tok % 64, 0)).astype(F32).astype(BF16)
    rows = jnp.concatenate([digits, a1, a2, a3, jnp.zeros((LANE - 4 * n_e, s_len), BF16)], axis=0)
    slot = lax.broadcasted_iota(jnp.int32, (s_len, cap), 1).astype(F32)
    for e in range(n_e):
        onehot = jnp.where(tok_major[:, e:e + 1] == slot, 1.0, 0.0).astype(BF16)
        res = jnp.dot(rows, onehot, preferred_element_type=F32)
        tok_idx = res[0:1, :] * 64.0 + res[1:2, :]
        gate = (res[n_e + e:n_e + e + 1, :] + res[2 * n_e + e:2 * n_e + e + 1, :]
                + res[3 * n_e + e:3 * n_e + e + 1, :])
        info_ref[0, e] = jnp.concatenate([tok_idx, gate, jnp.zeros((6, cap), F32)], axis=0)


def _route(logits_t, b, s, cap):
    n_e = logits_t.shape[0]
    kern = functools.partial(_route_kernel, cap=cap)
    return pl.pallas_call(
        kern,
        grid=(b,),
        in_specs=[pl.BlockSpec((n_e, s), lambda bi: (0, bi))],
        out_specs=[pl.BlockSpec((1, s, LANE), lambda bi: (bi, 0, 0)),
                   pl.BlockSpec((1, n_e, 8, cap), lambda bi: (bi, 0, 0, 0))],
        out_shape=[jax.ShapeDtypeStruct((b, s, LANE), jnp.int32),
                   jax.ShapeDtypeStruct((b, n_e, 8, cap), F32)],
        compiler_params=_cparams(("arbitrary",)),
        name="route",
    )(logits_t)


def _gather_kernel(rows_ref, src_ref, o_ref, buf_ref, sem, *, rows_per_step):
    i = pl.program_id(0)

    def issue(step, slot):
        base = step * rows_per_step

        def body(k, carry):
            pltpu.make_async_copy(src_ref.at[pl.ds(rows_ref[base + k], 1)],
                                  buf_ref.at[slot, pl.ds(k, 1)], sem.at[slot]).start()
            return carry

        lax.fori_loop(0, rows_per_step, body, 0, unroll=8)

    @pl.when(i == 0)
    def _():
        issue(0, 0)

    @pl.when(i + 1 < pl.num_programs(0))
    def _():
        issue(i + 1, (i + 1) % 2)

    slot = i % 2
    pltpu.make_async_copy(src_ref.at[pl.ds(0, rows_per_step)], buf_ref.at[slot], sem.at[slot]).wait()
    o_ref[...] = buf_ref[slot].astype(o_ref.dtype)


def _gather_rows(rows, src):
    n = rows.shape[0]
    d = src.shape[1]
    rows_per_step = min(n, 512)
    kern = functools.partial(_gather_kernel, rows_per_step=rows_per_step)
    return pl.pallas_call(
        kern,
        grid_spec=pltpu.PrefetchScalarGridSpec(
            num_scalar_prefetch=1,
            grid=(n // rows_per_step,),
            in_specs=[pl.BlockSpec(memory_space=pl.ANY)],
            out_specs=pl.BlockSpec((rows_per_step, d), lambda i, rows: (i, 0)),
            scratch_shapes=[pltpu.VMEM((2, rows_per_step, d), src.dtype), pltpu.SemaphoreType.DMA((2,))]),
        out_shape=jax.ShapeDtypeStruct((n, d), BF16),
        compiler_params=_cparams(("arbitrary",)),
        name="gather",
    )(rows, src)


def _ffn_kernel(x_ref, gate_ref, wg_ref, wu_ref, wd_ref, y_ref, acc_ref):
    f = pl.program_id(2)

    @pl.when(f == 0)
    def _():
        acc_ref[...] = jnp.zeros_like(acc_ref)

    xb = x_ref[...]
    a = jnp.dot(xb, wg_ref[0].astype(BF16), preferred_element_type=F32)
    u = jnp.dot(xb, wu_ref[0].astype(BF16), preferred_element_type=F32)
    hmid = (_silu(a) * u).astype(BF16)
    acc_ref[...] += jnp.dot(hmid, wd_ref[0].astype(BF16), preferred_element_type=F32)

    @pl.when(f == pl.num_programs(2) - 1)
    def _():
        y_ref[...] = (acc_ref[...] * gate_ref[...]).astype(BF16)


def _ffn(xg, gate_col, w_gate, w_up, w_down, rows_per_expert):
    n, d = xg.shape
    n_e, _, ff = w_gate.shape
    tm = min(rows_per_expert, 1024)
    tf = min(ff, 256)
    tiles_per_expert = rows_per_expert // tm
    row_map = lambda e, r, f: (e * tiles_per_expert + r, 0)
    return pl.pallas_call(
        _ffn_kernel,
        grid=(n_e, tiles_per_expert, ff // tf),
        in_specs=[pl.BlockSpec((tm, d), row_map), pl.BlockSpec((tm, 1), row_map),
                  pl.BlockSpec((1, d, tf), lambda e, r, f: (e, 0, f)),
                  pl.BlockSpec((1, d, tf), lambda e, r, f: (e, 0, f)),
                  pl.BlockSpec((1, tf, d), lambda e, r, f: (e, f, 0))],
        out_specs=pl.BlockSpec((tm, d), row_map),
        out_shape=jax.ShapeDtypeStruct((n, d), BF16),
        scratch_shapes=[pltpu.VMEM((tm, d), F32)],
        compiler_params=_cparams(("arbitrary", "arbitrary", "arbitrary")),
        name="ffn",
    )(xg, gate_col, w_gate, w_up, w_down)


def _combine_kernel(post_ref, y_ref, x1_ref, g2_ref, gpost_ref, o_ref, acc_ref):
    e = pl.program_id(2)

    @pl.when(e == 0)
    def _():
        acc_ref[...] = jnp.zeros_like(acc_ref)

    posmat = post_ref[0]
    lane = lax.broadcasted_iota(jnp.int32, posmat.shape, 1)
    pos = jnp.sum(jnp.where(lane == e, posmat, 0), axis=1, keepdims=True)
    cap = y_ref.shape[0]
    slot = lax.broadcasted_iota(jnp.int32, (pos.shape[0], cap), 1)
    onehot = jnp.where(pos == slot, 1.0, 0.0).astype(BF16)
    acc_ref[...] += jnp.dot(onehot, y_ref[...], preferred_element_type=F32)

    @pl.when(e == pl.num_programs(2) - 1)
    def _():
        o_ref[...] = x1_ref[...] + g2_ref[0] * (_rms(acc_ref[...]) * gpost_ref[...])


def _combine(pos_t, y, x1, gate2, gpost, b, s, cap):
    t, d = x1.shape
    n_e = y.shape[0] // (b * cap)
    tt = min(s, 1024)
    tiles_per_seq = s // tt
    return pl.pallas_call(
        _combine_kernel,
        grid=(b, tiles_per_seq, n_e),
        in_specs=[pl.BlockSpec((1, tt, LANE), lambda bi, ji, e: (bi, ji, 0)),
                  pl.BlockSpec((cap, d), lambda bi, ji, e: (e * b + bi, 0)),
                  pl.BlockSpec((tt, d), lambda bi, ji, e: (bi * tiles_per_seq + ji, 0)),
                  pl.BlockSpec((1, 1, d), lambda bi, ji, e: (bi, 0, 0)),
                  pl.BlockSpec((1, d), lambda bi, ji, e: (0, 0))],
        out_specs=pl.BlockSpec((tt, d), lambda bi, ji, e: (bi * tiles_per_seq + ji, 0)),
        out_shape=jax.ShapeDtypeStruct((t, d), F32),
        scratch_shapes=[pltpu.VMEM((tt, d), F32)],
        compiler_params=_cparams(("arbitrary", "arbitrary", "arbitrary")),
        name="combine",
    )(pos_t, y, x1, gate2, gpost)


def kernel(x, c, positions, w_ada, b_ada, g_pre_mix, g_post_mix, w_in, diff_lambda_q1, diff_lambda_k1,
           diff_lambda_q2, diff_lambda_k2, diff_subln_gain, ret_decay_logit, ret_norm_gain, w_out, g_pre_ffn,
           g_post_ffn, w_router, w_gate, w_up, w_down):
    b, s, d = x.shape
    t = b * s
    depth = w_ada.shape[0]
    n_e = w_router.shape[2]
    cap = EC_CAPACITY_FACTOR * s // n_e
    diff_w = DIFF_HEADS * LANE

    pos_col = positions.reshape(t, 1)
    tabs = _rope_tables(pos_col)
    c_pad = jnp.pad(c, ((0, (-b) % 8), (0, 0)))
    x2 = x.reshape(t, d)
    for l in range(depth):
        lam_init = 0.8 - 0.6 * math.exp(-0.3 * l)
        mod = _ada(c_pad, w_ada[l], b_ada[l].reshape(1, -1))[:b]
        shift1, scale1, gate1, shift2, scale2, gate2 = [m.reshape(b, 1, d) for m in jnp.split(mod, 6, axis=-1)]

        proj = _inproj(x2, scale1, shift1, g_pre_mix[l].reshape(1, d), w_in[l].astype(BF16), tabs, s)
        proj3 = proj.reshape(b, s, -1)
        lams = [v[l].reshape(1, -1) for v in (diff_lambda_q1, diff_lambda_k1, diff_lambda_q2, diff_lambda_k2)]
        ad = _diffattn(proj3, lams, diff_subln_gain[l].reshape(1, -1), lam_init)
        ro = _retention(proj3, ret_decay_logit[l], ret_norm_gain[l].reshape(1, -1))

        w_out_bf = w_out[l].astype(BF16)
        x1, h2, logits_t = _outproj(
            ad.reshape(t, -1), ro.reshape(t, -1), w_out_bf[:diff_w], w_out_bf[diff_w:], x2, gate1, scale2,
            shift2, g_post_mix[l].reshape(1, d), g_pre_ffn[l].reshape(1, d), w_router[l].T, s)

        pos_t, info = _route(logits_t, b, s, cap)
        rows = info[:, :, 0, :].astype(jnp.int32) + (jnp.arange(b, dtype=jnp.int32) * s)[:, None, None]
        rows = jnp.transpose(rows, (1, 0, 2)).reshape(-1)
        gate_col = jnp.transpose(info[:, :, 1, :], (1, 0, 2)).reshape(-1, 1)
        xg = _gather_rows(rows, h2)
        y = _ffn(xg, gate_col, w_gate[l], w_up[l], w_down[l], b * cap)
        x2 = _combine(pos_t, y, x1, gate2, g_post_ffn[l].reshape(1, d), b, s, cap)
    return x2.reshape(b, s, d)
```

```python
import functools
import math

import jax
import jax.numpy as jnp
from jax import lax
from jax.experimental import pallas as pl
from jax.experimental.pallas import tpu as pltpu

F32 = jnp.float32
BF16 = jnp.bfloat16
HIGHEST = lax.Precision.HIGHEST

DIFF_HEADS = 8
DIFF_HEAD_DIM = 64
RET_HEADS = 8
RET_QK_DIM = 128
LANE = 128
N_EXPERTS = 16
EC_CAPACITY_FACTOR = 2
ROPE_THETA = 10000.0
NORM_EPS = 1e-6
RET_CHUNK = 128
VMEM_LIMIT = 56 * 1024 * 1024


def _cparams(sem):
    return pltpu.CompilerParams(dimension_semantics=sem, vmem_limit_bytes=VMEM_LIMIT)


def _silu(x):
    return x / (1.0 + jnp.exp(-x))


def _rms(x, eps=NORM_EPS):
    return x * lax.rsqrt(jnp.mean(x * x, axis=-1, keepdims=True) + eps)


def _ada_kernel(c_ref, w_ref, b_ref, o_ref):
    s = _silu(c_ref[...])
    o_ref[...] = jnp.dot(s, w_ref[...], precision=HIGHEST, preferred_element_type=F32) + b_ref[...]


def _ada(c_pad, w_ada, b_ada):
    m, d = c_pad.shape
    n = w_ada.shape[1]
    tn = min(d, 1024)
    assert n % tn == 0
    return pl.pallas_call(
        _ada_kernel,
        grid=(n // tn,),
        in_specs=[pl.BlockSpec((m, d), lambda j: (0, 0)),
                  pl.BlockSpec((d, tn), lambda j: (0, j)),
                  pl.BlockSpec((1, tn), lambda j: (0, j))],
        out_specs=pl.BlockSpec((m, tn), lambda j: (0, j)),
        out_shape=jax.ShapeDtypeStruct((m, n), F32),
        compiler_params=_cparams(("arbitrary",)),
        name="ada",
    )(c_pad, w_ada, b_ada)


def _rope_tab_kernel(pos_ref, cd_ref, sda_ref, sdb_ref, cr_ref, sr_ref):
    pos = pos_ref[...].astype(F32)
    lane = lax.broadcasted_iota(jnp.int32, (pos.shape[0], LANE), 1)
    is_d = lane < 32
    expo = jnp.where(is_d, lane.astype(F32) * (2.0 / DIFF_HEAD_DIM),
                     (lane - 32).astype(F32) * (2.0 / RET_QK_DIM))
    inv = jnp.exp(expo * (-math.log(ROPE_THETA)))
    ang = pos * inv
    c = jnp.cos(ang)
    s = jnp.sin(ang)
    c32, c64, c96 = pltpu.roll(c, 32, 1), pltpu.roll(c, 64, 1), pltpu.roll(c, 96, 1)
    s32, s64, s96 = pltpu.roll(s, 32, 1), pltpu.roll(s, 64, 1), pltpu.roll(s, 96, 1)
    q = lane // 32
    cos_d = jnp.where(q == 0, c, jnp.where(q == 1, c32, jnp.where(q == 2, c64, c96)))
    sin_d = jnp.where(q == 0, s, jnp.where(q == 1, s32, jnp.where(q == 2, s64, s96)))
    first_half = (q == 0) | (q == 2)
    cd_ref[...] = cos_d
    sda_ref[...] = jnp.where(first_half, -sin_d, 0.0)
    sdb_ref[...] = jnp.where(first_half, 0.0, sin_d)
    lo = lane < 64
    cr_ref[...] = jnp.where(lo, c96, c32)
    sr_ref[...] = jnp.where(lo, -s96, s32)


def _rope_tables(pos_col):
    t = pos_col.shape[0]
    tm = min(t, 2048)
    spec = pl.BlockSpec((tm, LANE), lambda i: (i, 0))
    return pl.pallas_call(
        _rope_tab_kernel,
        grid=(t // tm,),
        in_specs=[pl.BlockSpec((tm, 1), lambda i: (i, 0))],
        out_specs=[spec] * 5,
        out_shape=[jax.ShapeDtypeStruct((t, LANE), F32)] * 5,
        compiler_params=_cparams(("arbitrary",)),
        name="rope_tab",
    )(pos_col)


def _inproj_kernel(x_ref, sc_ref, sh_ref, g_ref, w_ref, cd_ref, sda_ref, sdb_ref, cr_ref, sr_ref,
                   o_ref, h_ref, acc_ref, *, n_col, q_scale, rk_scale):
    j = pl.program_id(1)
    nslab = o_ref.shape[1] // LANE

    def epilogue(kind, slot):
        if kind in (0, 1):
            scale = q_scale if kind == 0 else 1.0
            cd, sa, sb = cd_ref[...], sda_ref[...], sdb_ref[...]
            for s in range(nslab):
                xs = acc_ref[slot, :, s * LANE:(s + 1) * LANE]
                y = xs * cd + pltpu.roll(xs, 96, 1) * sa + pltpu.roll(xs, 32, 1) * sb
                o_ref[:, s * LANE:(s + 1) * LANE] = (y * scale).astype(BF16)
        elif kind in (3, 4):
            scale = rk_scale if kind == 4 else 1.0
            cr, sr = cr_ref[...], sr_ref[...]
            for s in range(nslab):
                xs = acc_ref[slot, :, s * LANE:(s + 1) * LANE]
                y = xs * cr + pltpu.roll(xs, 64, 1) * sr
                o_ref[:, s * LANE:(s + 1) * LANE] = (y * scale).astype(BF16)
        else:
            o_ref[...] = acc_ref[slot].astype(BF16)

    for jj in range(n_col + 1):
        @pl.when(j == jj)
        def _(jj=jj):
            if jj == 0:
                h = _rms(x_ref[...]) * g_ref[...] * (1.0 + sc_ref[0]) + sh_ref[0]
                h_ref[...] = h.astype(BF16)
            if jj < n_col:
                acc_ref[jj % 2] = jnp.dot(h_ref[...], w_ref[...], preferred_element_type=F32)
            if jj > 0:
                epilogue(jj - 1, (jj - 1) % 2)


def _inproj(x2, scale1, shift1, g, w_bf, tabs, seq):
    t, d = x2.shape
    nc = w_bf.shape[1]
    n_col = 7
    tn = nc // n_col
    tm = min(seq, 1024)
    tiles_per_seq = seq // tm
    kern = functools.partial(
        _inproj_kernel, n_col=n_col,
        q_scale=(DIFF_HEAD_DIM ** -0.5) * math.log2(math.e), rk_scale=RET_QK_DIM ** -0.5)
    mod_spec = pl.BlockSpec((1, 1, d), lambda i, j: (i // tiles_per_seq, 0, 0))
    tab_spec = pl.BlockSpec((tm, LANE), lambda i, j: (i, 0))
    return pl.pallas_call(
        kern,
        grid=(t // tm, n_col + 1),
        in_specs=[pl.BlockSpec((tm, d), lambda i, j: (i, 0)), mod_spec, mod_spec,
                  pl.BlockSpec((1, d), lambda i, j: (0, 0)),
                  pl.BlockSpec((d, tn), lambda i, j: (0, jnp.minimum(j, n_col - 1)))] + [tab_spec] * 5,
        out_specs=pl.BlockSpec((tm, tn), lambda i, j: (i, jnp.maximum(j - 1, 0))),
        out_shape=jax.ShapeDtypeStruct((t, nc), BF16),
        scratch_shapes=[pltpu.VMEM((tm, d), BF16), pltpu.VMEM((2, tm, tn), F32)],
        compiler_params=_cparams(("arbitrary", "arbitrary")),
        name="inproj",
    )(x2, scale1, shift1, g, w_bf, *tabs)


def _diffattn_kernel(lq1_ref, lk1_ref, lq2_ref, lk2_ref, q_ref, k_ref, v_ref, g_ref, o_ref,
                     vt_ref, s_ref, mx_ref, out0_ref, *, lam_init, tk):
    i = pl.program_id(2)
    last = pl.num_programs(2) - 1
    dv = v_ref.shape[2]
    tq = q_ref.shape[1]
    s_len = k_ref.shape[1]
    nk = s_len // tk

    @pl.when(i == 0)
    def _():
        extra = lax.broadcasted_iota(jnp.int32, (vt_ref.shape[0] - dv, s_len), 0)
        vt_ref[0:dv, :] = v_ref[0].astype(F32).T.astype(BF16)
        vt_ref[dv:, :] = jnp.where(extra == 0, 1.0, 0.0).astype(BF16)
        s_ref[1] = jnp.zeros(s_ref.shape[1:], F32)
        mx_ref[1] = jnp.zeros(mx_ref.shape[1:], F32)

    q = q_ref[0]
    lane = lax.broadcasted_iota(jnp.int32, q.shape, 1)

    def phase(score_map, value_map):
        lo = score_map * DIFF_HEAD_DIM
        qm = jnp.where((lane >= lo) & (lane < lo + DIFF_HEAD_DIM), q, jnp.zeros_like(q))
        mx = jnp.max(mx_ref[value_map], axis=0, keepdims=True)
        cmax = jnp.full((8, tq), -jnp.inf, F32)
        acc = jnp.zeros((vt_ref.shape[0], tq), F32)
        for c in range(nk):
            rows = slice(c * tk, (c + 1) * tk)
            sc = lax.dot_general(k_ref[0, rows, :], qm, (((1,), (1,)), ((), ())),
                                 preferred_element_type=F32)
            s_ref[score_map, rows, :] = sc
            cmax = jnp.maximum(cmax, jnp.max(sc.reshape(tk // 8, 8, tq), axis=0))
            p = jnp.exp2((s_ref[value_map, rows, :] - mx).astype(BF16))
            acc = acc + jnp.dot(vt_ref[:, rows], p, preferred_element_type=F32)
        mx_ref[score_map] = cmax
        return acc[0:dv, :] / acc[dv:dv + 1, :]

    res1 = phase(0, 1)

    @pl.when(i > 0)
    def _():
        lam = (jnp.exp(jnp.sum(lq1_ref[...] * lk1_ref[...], axis=-1, keepdims=True))
               - jnp.exp(jnp.sum(lq2_ref[...] * lk2_ref[...], axis=-1, keepdims=True)) + lam_init)
        ad = out0_ref[...] - lam * res1
        y = ad * lax.rsqrt(jnp.mean(ad * ad, axis=0, keepdims=True) + NORM_EPS)
        o_ref[0] = (y.T * (g_ref[...] * (1.0 - lam_init))).astype(BF16)

    @pl.when(i < last)
    def _():
        out0_ref[...] = phase(1, 0)


def _diffattn(proj3, lams, subln_gain, lam_init):
    b, s, _ = proj3.shape
    h = DIFF_HEADS
    tq = min(s, 512)
    tk = min(s, 512)
    nq = s // tq
    kern = functools.partial(_diffattn_kernel, lam_init=lam_init, tk=tk)
    lam_spec = pl.BlockSpec((1, DIFF_HEAD_DIM), lambda bi, hi, i: (0, 0))
    return pl.pallas_call(
        kern,
        grid=(b, h, nq + 1),
        in_specs=[lam_spec] * 4 + [
            pl.BlockSpec((1, tq, LANE), lambda bi, hi, i: (bi, jnp.minimum(i, nq - 1), hi)),
            pl.BlockSpec((1, s, LANE), lambda bi, hi, i: (bi, 0, h + hi)),
            pl.BlockSpec((1, s, LANE), lambda bi, hi, i: (bi, 0, 2 * h + hi)),
            pl.BlockSpec((1, LANE), lambda bi, hi, i: (0, 0))],
        out_specs=pl.BlockSpec((1, tq, LANE), lambda bi, hi, i: (bi, jnp.maximum(i - 1, 0), hi)),
        out_shape=jax.ShapeDtypeStruct((b, s, h * LANE), BF16),
        scratch_shapes=[pltpu.VMEM((LANE + 16, s), BF16), pltpu.VMEM((2, s, tq), F32),
                        pltpu.VMEM((2, 8, tq), F32), pltpu.VMEM((LANE, tq), F32)],
        compiler_params=_cparams(("arbitrary", "arbitrary", "arbitrary")),
        name="diffattn",
    )(*lams, proj3, proj3, proj3, subln_gain)


def _retention_kernel(logit_ref, q_ref, k_ref, v_ref, gate_ref, gain_ref, o_ref, acc_ref, accb_ref, *, chunk):
    hi = pl.program_id(1)
    c_len = chunk
    s_len = q_ref.shape[1]
    n_chunks = s_len // c_len
    lane = lax.broadcasted_iota(jnp.int32, (1, logit_ref.shape[1]), 1)
    lg = logit_ref[...]
    log_g = -jnp.log(1.0 + jnp.exp(-lg))
    log_g = jnp.sum(jnp.where(lane == hi, log_g, 0.0), axis=-1, keepdims=True)
    lgf, lgb = log_g[0:1, :], log_g[1:2, :]

    ii = lax.broadcasted_iota(jnp.int32, (c_len, c_len), 0).astype(F32)
    jj = lax.broadcasted_iota(jnp.int32, (c_len, c_len), 1).astype(F32)
    dist = ii - jj
    decay = jnp.where(dist >= 0, jnp.exp(lgf * jnp.maximum(dist, 0.0)),
                      jnp.exp(lgb * jnp.maximum(-dist, 0.0)))
    ci = lax.broadcasted_iota(jnp.int32, (c_len, 1), 0).astype(F32)
    qdec_f = jnp.exp(lgf * (ci + 1.0))
    kdec_f = jnp.exp(lgf * (c_len - 1.0 - ci))
    qdec_b = jnp.exp(lgb * (c_len - ci))
    kdec_b = jnp.exp(lgb * ci)
    cdec_f = jnp.exp(lgf * c_len)
    cdec_b = jnp.exp(lgb * c_len)
    dk = q_ref.shape[2]
    dv = v_ref.shape[2]

    def load(c):
        r = pl.ds(pl.multiple_of(c * c_len, c_len), c_len)
        return q_ref[0, r, :], k_ref[0, r, :], v_ref[0, r, :], r

    def fwd(c, state):
        q, k, v, r = load(c)
        qf, kf = q.astype(F32), k.astype(F32)
        sc = lax.dot_general(q, k, (((1,), (1,)), ((), ())), preferred_element_type=F32) * decay
        inner = jnp.dot(sc.astype(BF16), v, preferred_element_type=F32)
        cross = jnp.dot((qf * qdec_f).astype(BF16), state.astype(BF16), preferred_element_type=F32)
        acc_ref[r, :] = inner + cross
        kv = lax.dot_general((kf * kdec_f).astype(BF16), v, (((0,), (0,)), ((), ())),
                             preferred_element_type=F32)
        return state * cdec_f + kv

    def bwd(c, state):
        q, k, v, r = load(c)
        qf, kf = q.astype(F32), k.astype(F32)
        accb_ref[r, :] = jnp.dot((qf * qdec_b).astype(BF16), state.astype(BF16), preferred_element_type=F32)
        kv = lax.dot_general((kf * kdec_b).astype(BF16), v, (((0,), (0,)), ((), ())),
                             preferred_element_type=F32)
        return state * cdec_b + kv

    def both(i, states):
        return fwd(i, states[0]), bwd(n_chunks - 1 - i, states[1])

    zero = jnp.zeros((dk, dv), F32)
    lax.fori_loop(0, n_chunks, both, (zero, zero), unroll=2)

    ro = acc_ref[...] + accb_ref[...]
    mu = jnp.mean(ro, axis=-1, keepdims=True)
    xc = ro - mu
    y = xc * lax.rsqrt(jnp.mean(xc * xc, axis=-1, keepdims=True) + NORM_EPS) * gain_ref[...]
    o_ref[0] = (_silu(gate_ref[0].astype(F32)) * y).astype(BF16)


def _retention(proj3, decay_logit, norm_gain):
    b, s, _ = proj3.shape
    h = RET_HEADS
    base = 3 * DIFF_HEADS
    kern = functools.partial(_retention_kernel, chunk=min(RET_CHUNK, s))

    def col(off):
        return pl.BlockSpec((1, s, LANE), lambda bi, hi: (bi, 0, base + off * h + hi))

    return pl.pallas_call(
        kern,
        grid=(b, h),
        in_specs=[pl.BlockSpec((2, h), lambda bi, hi: (0, 0)), col(0), col(1), col(2), col(3),
                  pl.BlockSpec((1, LANE), lambda bi, hi: (0, hi))],
        out_specs=pl.BlockSpec((1, s, LANE), lambda bi, hi: (bi, 0, hi)),
        out_shape=jax.ShapeDtypeStruct((b, s, h * LANE), BF16),
        scratch_shapes=[pltpu.VMEM((s, LANE), F32), pltpu.VMEM((s, LANE), F32)],
        compiler_params=_cparams(("arbitrary", "arbitrary")),
        name="retention",
    )(decay_logit, proj3, proj3, proj3, proj3, norm_gain)


def _outproj_kernel(ad_ref, ro_ref, wa_ref, wr_ref, x_ref, g1_ref, sc2_ref, sh2_ref, gpost_ref, gpre_ref,
                    wrt_ref, x1_ref, h2_ref, lt_ref):
    mix = (jnp.dot(ad_ref[...], wa_ref[...], preferred_element_type=F32)
           + jnp.dot(ro_ref[...], wr_ref[...], preferred_element_type=F32))
    x1 = x_ref[...] + g1_ref[0] * (_rms(mix) * gpost_ref[...])
    x1_ref[...] = x1
    h2 = _rms(x1) * gpre_ref[...] * (1.0 + sc2_ref[0]) + sh2_ref[0]
    h2_ref[...] = h2
    lt_ref[...] = lax.dot_general(wrt_ref[...], h2, (((1,), (1,)), ((), ())), precision=HIGHEST,
                                  preferred_element_type=F32)


def _outproj(ad2, ro2, wa_bf, wr_bf, x2, gate1, scale2, shift2, gpost, gpre, w_router_t, seq):
    t, d = x2.shape
    ka, kr = ad2.shape[1], ro2.shape[1]
    e = w_router_t.shape[0]
    tm = min(seq, 512)
    tiles_per_seq = seq // tm
    mod_spec = pl.BlockSpec((1, 1, d), lambda i: (i // tiles_per_seq, 0, 0))
    vec_spec = pl.BlockSpec((1, d), lambda i: (0, 0))
    row_spec = pl.BlockSpec((tm, d), lambda i: (i, 0))
    return pl.pallas_call(
        _outproj_kernel,
        grid=(t // tm,),
        in_specs=[pl.BlockSpec((tm, ka), lambda i: (i, 0)), pl.BlockSpec((tm, kr), lambda i: (i, 0)),
                  pl.BlockSpec((ka, d), lambda i: (0, 0)), pl.BlockSpec((kr, d), lambda i: (0, 0)),
                  row_spec, mod_spec, mod_spec, mod_spec, vec_spec, vec_spec,
                  pl.BlockSpec((e, d), lambda i: (0, 0))],
        out_specs=[row_spec, row_spec, pl.BlockSpec((e, tm), lambda i: (0, i))],
        out_shape=[jax.ShapeDtypeStruct((t, d), F32), jax.ShapeDtypeStruct((t, d), F32),
                   jax.ShapeDtypeStruct((e, t), F32)],
        compiler_params=_cparams(("arbitrary",)),
        name="outproj",
    )(ad2, ro2, wa_bf, wr_bf, x2, gate1, scale2, shift2, gpost, gpre, w_router_t)


def _lane_cumsum_exclusive(x, tri):
    r, s = x.shape
    carry = jnp.zeros((r, 1), F32)
    cols = []
    for blk in range(s // LANE):
        xb = x[:, blk * LANE:(blk + 1) * LANE]
        inc = jnp.dot(xb.astype(BF16), tri, preferred_element_type=F32)
        cols.append(inc - xb + carry)
        carry = carry + inc[:, LANE - 1:LANE]
    return jnp.concatenate(cols, axis=1)


def _route_kernel(lt_ref, post_ref, info_ref, *, cap):
    logits = lt_ref[...]
    n_e, s_len = logits.shape
    mx = jnp.max(logits, axis=0, keepdims=True)
    ex = jnp.exp(logits - mx)
    aff = ex / jnp.sum(ex, axis=0, keepdims=True)

    kf = float(cap)
    tbits = jnp.zeros((n_e, 1), jnp.int32)
    for bit in range(30, -1, -1):
        cand = tbits | (1 << bit)
        cnt = jnp.sum(jnp.where(aff >= pltpu.bitcast(cand, F32), 1.0, 0.0), axis=1, keepdims=True)
        tbits = jnp.where(cnt >= kf, cand, tbits)
    thr = pltpu.bitcast(tbits, F32)

    ri = lax.broadcasted_iota(jnp.int32, (LANE, LANE), 0)
    rj = lax.broadcasted_iota(jnp.int32, (LANE, LANE), 1)
    tri = jnp.where(ri <= rj, 1.0, 0.0).astype(BF16)
    gt = jnp.where(aff > thr, 1.0, 0.0)
    tie = jnp.where(aff == thr, 1.0, 0.0)
    need = kf - jnp.sum(gt, axis=1, keepdims=True)
    tie_rank = _lane_cumsum_exclusive(tie, tri)
    sel = gt + tie * jnp.where(tie_rank < need, 1.0, 0.0)
    pos = _lane_cumsum_exclusive(sel, tri)
    pos = jnp.where(sel > 0.5, pos, -1.0)

    tok_major = jnp.concatenate([pos, jnp.zeros((LANE - n_e, s_len), F32)], axis=0).T
    post_ref[0] = tok_major.astype(jnp.int32)
    a1 = aff.astype(BF16)
    r1 = aff - a1.astype(F32)
    a2 = r1.astype(BF16)
    a3 = (r1 - a2.astype(F32)).astype(BF16)
    tok = lax.broadcasted_iota(jnp.int32, (n_e, s_len), 1)
    row = lax.broadcasted_iota(jnp.int32, (n_e, s_len), 0)
    digits = jnp.where(row == 0, tok // 64, jnp.where(row == 1, tok % 64, 0)).astype(F32).astype(BF16)
    rows = jnp.concatenate([digits, a1, a2, a3, jnp.zeros((LANE - 4 * n_e, s_len), BF16)], axis=0)
    slot = lax.broadcasted_iota(jnp.int32, (s_len, cap), 1).astype(F32)
    for e in range(n_e):
        onehot = jnp.where(tok_major[:, e:e + 1] == slot, 1.0, 0.0).astype(BF16)
        res = jnp.dot(rows, onehot, preferred_element_type=F32)
        tok_idx = res[0:1, :] * 64.0 + res[1:2, :]
        gate = (res[n_e + e:n_e + e + 1, :] + res[2 * n_e + e:2 * n_e + e + 1, :]
                + res[3 * n_e + e:3 * n_e + e + 1, :])
        info_ref[0, e] = jnp.concatenate([tok_idx, gate, jnp.zeros((6, cap), F32)], axis=0)


def _route(logits_t, b, s, cap):
    n_e = logits_t.shape[0]
    kern = functools.partial(_route_kernel, cap=cap)
    return pl.pallas_call(
        kern,
        grid=(b,),
        in_specs=[pl.BlockSpec((n_e, s), lambda bi: (0, bi))],
        out_specs=[pl.BlockSpec((1, s, LANE), lambda bi: (bi, 0, 0)),
                   pl.BlockSpec((1, n_e, 8, cap), lambda bi: (bi, 0, 0, 0))],
        out_shape=[jax.ShapeDtypeStruct((b, s, LANE), jnp.int32),
                   jax.ShapeDtypeStruct((b, n_e, 8, cap), F32)],
        compiler_params=_cparams(("arbitrary",)),
        name="route",
    )(logits_t)


def _gather_kernel(rows_ref, src_ref, o_ref, buf_ref, sem, *, rows_per_step):
    i = pl.program_id(0)

    def issue(step, slot):
        base = step * rows_per_step

        def body(k, carry):
            pltpu.make_async_copy(src_ref.at[pl.ds(rows_ref[base + k], 1)],
                                  buf_ref.at[slot, pl.ds(k, 1)], sem.at[slot]).start()
            return carry

        lax.fori_loop(0, rows_per_step, body, 0, unroll=8)

    @pl.when(i == 0)
    def _():
        issue(0, 0)

    @pl.when(i + 1 < pl.num_programs(0))
    def _():
        issue(i + 1, (i + 1) % 2)

    slot = i % 2
    pltpu.make_async_copy(src_ref.at[pl.ds(0, rows_per_step)], buf_ref.at[slot], sem.at[slot]).wait()
    o_ref[...] = buf_ref[slot].astype(o_ref.dtype)


def _gather_rows(rows, src):
    n = rows.shape[0]
    d = src.shape[1]
    rows_per_step = min(n, 512)
    kern = functools.partial(_gather_kernel, rows_per_step=rows_per_step)
    return pl.pallas_call(
        kern,
        grid_spec=pltpu.PrefetchScalarGridSpec(
            num_scalar_prefetch=1,
            grid=(n // rows_per_step,),
            in_specs=[pl.BlockSpec(memory_space=pl.ANY)],
            out_specs=pl.BlockSpec((rows_per_step, d), lambda i, rows: (i, 0)),
            scratch_shapes=[pltpu.VMEM((2, rows_per_step, d), src.dtype), pltpu.SemaphoreType.DMA((2,))]),
        out_shape=jax.ShapeDtypeStruct((n, d), BF16),
        compiler_params=_cparams(("arbitrary",)),
        name="gather",
    )(rows, src)


def _ffn_kernel(x_ref, gate_ref, wg_ref, wu_ref, wd_ref, y_ref, acc_ref):
    f = pl.program_id(2)

    @pl.when(f == 0)
    def _():
        acc_ref[...] = jnp.zeros_like(acc_ref)

    xb = x_ref[...]
    a = jnp.dot(xb, wg_ref[0].astype(BF16), preferred_element_type=F32)
    u = jnp.dot(xb, wu_ref[0].astype(BF16), preferred_element_type=F32)
    hmid = (_silu(a) * u).astype(BF16)
    acc_ref[...] += jnp.dot(hmid, wd_ref[0].astype(BF16), preferred_element_type=F32)

    @pl.when(f == pl.num_programs(2) - 1)
    def _():
        y_ref[...] = (acc_ref[...] * gate_ref[...]).astype(BF16)


def _ffn(xg, gate_col, w_gate, w_up, w_down, rows_per_expert):
    n, d = xg.shape
    n_e, _, ff = w_gate.shape
    tm = min(rows_per_expert, 1024)
    tf = min(ff, 512)
    tiles_per_expert = rows_per_expert // tm
    row_map = lambda e, r, f: (e * tiles_per_expert + r, 0)
    return pl.pallas_call(
        _ffn_kernel,
        grid=(n_e, tiles_per_expert, ff // tf),
        in_specs=[pl.BlockSpec((tm, d), row_map), pl.BlockSpec((tm, 1), row_map),
                  pl.BlockSpec((1, d, tf), lambda e, r, f: (e, 0, f)),
                  pl.BlockSpec((1, d, tf), lambda e, r, f: (e, 0, f)),
                  pl.BlockSpec((1, tf, d), lambda e, r, f: (e, f, 0))],
        out_specs=pl.BlockSpec((tm, d), row_map),
        out_shape=jax.ShapeDtypeStruct((n, d), BF16),
        scratch_shapes=[pltpu.VMEM((tm, d), F32)],
        compiler_params=_cparams(("arbitrary", "arbitrary", "arbitrary")),
        name="ffn",
    )(xg, gate_col, w_gate, w_up, w_down)


def _combine_kernel(post_ref, y_ref, x1_ref, g2_ref, gpost_ref, o_ref, acc_ref):
    e = pl.program_id(2)

    @pl.when(e == 0)
    def _():
        acc_ref[...] = jnp.zeros_like(acc_ref)

    posmat = post_ref[0]
    lane = lax.broadcasted_iota(jnp.int32, posmat.shape, 1)
    pos = jnp.sum(jnp.where(lane == e, posmat, 0), axis=1, keepdims=True)
    cap = y_ref.shape[0]
    kb = min(cap, 256)
    hi = jnp.max(pos)
    lo = jnp.min(jnp.where(pos >= 0, pos, cap))
    for blk in range(cap // kb):
        @pl.when((hi >= blk * kb) & (lo < (blk + 1) * kb))
        def _():
            slot = lax.broadcasted_iota(jnp.int32, (pos.shape[0], kb), 1) + blk * kb
            onehot = jnp.where(pos == slot, 1.0, 0.0).astype(BF16)
            acc_ref[...] += jnp.dot(onehot, y_ref[blk * kb:(blk + 1) * kb, :], preferred_element_type=F32)

    @pl.when(e == pl.num_programs(2) - 1)
    def _():
        o_ref[...] = x1_ref[...] + g2_ref[0] * (_rms(acc_ref[...]) * gpost_ref[...])


def _combine(pos_t, y, x1, gate2, gpost, b, s, cap):
    t, d = x1.shape
    n_e = y.shape[0] // (b * cap)
    tt = min(s, 1024)
    tiles_per_seq = s // tt
    return pl.pallas_call(
        _combine_kernel,
        grid=(b, tiles_per_seq, n_e),
        in_specs=[pl.BlockSpec((1, tt, LANE), lambda bi, ji, e: (bi, ji, 0)),
                  pl.BlockSpec((cap, d), lambda bi, ji, e: (e * b + bi, 0)),
                  pl.BlockSpec((tt, d), lambda bi, ji, e: (bi * tiles_per_seq + ji, 0)),
                  pl.BlockSpec((1, 1, d), lambda bi, ji, e: (bi, 0, 0)),
                  pl.BlockSpec((1, d), lambda bi, ji, e: (0, 0))],
        out_specs=pl.BlockSpec((tt, d), lambda bi, ji, e: (bi * tiles_per_seq + ji, 0)),
        out_shape=jax.ShapeDtypeStruct((t, d), F32),
        scratch_shapes=[pltpu.VMEM((tt, d), F32)],
        compiler_params=_cparams(("arbitrary", "arbitrary", "arbitrary")),
        name="combine",
    )(pos_t, y, x1, gate2, gpost)


def kernel(x, c, positions, w_ada, b_ada, g_pre_mix, g_post_mix, w_in, diff_lambda_q1, diff_lambda_k1,
           diff_lambda_q2, diff_lambda_k2, diff_subln_gain, ret_decay_logit, ret_norm_gain, w_out, g_pre_ffn,
           g_post_ffn, w_router, w_gate, w_up, w_down):
    b, s, d = x.shape
    t = b * s
    depth = w_ada.shape[0]
    n_e = w_router.shape[2]
    cap = EC_CAPACITY_FACTOR * s // n_e
    diff_w = DIFF_HEADS * LANE

    pos_col = positions.reshape(t, 1)
    tabs = _rope_tables(pos_col)
    c_pad = jnp.pad(c, ((0, (-b) % 8), (0, 0)))
    x2 = x.reshape(t, d)
    for l in range(depth):
        lam_init = 0.8 - 0.6 * math.exp(-0.3 * l)
        mod = _ada(c_pad, w_ada[l], b_ada[l].reshape(1, -1))[:b]
        shift1, scale1, gate1, shift2, scale2, gate2 = [m.reshape(b, 1, d) for m in jnp.split(mod, 6, axis=-1)]

        proj = _inproj(x2, scale1, shift1, g_pre_mix[l].reshape(1, d), w_in[l].astype(BF16), tabs, s)
        proj3 = proj.reshape(b, s, -1)
        lams = [v[l].reshape(1, -1) for v in (diff_lambda_q1, diff_lambda_k1, diff_lambda_q2, diff_lambda_k2)]
        ad = _diffattn(proj3, lams, diff_subln_gain[l].reshape(1, -1), lam_init)
        ro = _retention(proj3, ret_decay_logit[l], ret_norm_gain[l].reshape(1, -1))

        w_out_bf = w_out[l].astype(BF16)
        x1, h2, logits_t = _outproj(
            ad.reshape(t, -1), ro.reshape(t, -1), w_out_bf[:diff_w], w_out_bf[diff_w:], x2, gate1, scale2,
            shift2, g_post_mix[l].reshape(1, d), g_pre_ffn[l].reshape(1, d), w_router[l].T, s)

        pos_t, info = _route(logits_t, b, s, cap)
        rows = info[:, :, 0, :].astype(jnp.int32) + (jnp.arange(b, dtype=jnp.int32) * s)[:, None, None]
        rows = jnp.transpose(rows, (1, 0, 2)).reshape(-1)
        gate_col = jnp.transpose(info[:, :, 1, :], (1, 0, 2)).reshape(-1, 1)
        xg = _gather_rows(rows, h2)
        y = _ffn(xg, gate_col, w_gate[l], w_up[l], w_down[l], b * cap)
        x2 = _combine(pos_t, y, x1, gate2, g_post_ffn[l].reshape(1, d), b, s, cap)
    return x2.reshape(b, s, d)
```

```python
import functools
import math

import jax
import jax.numpy as jnp
from jax import lax
from jax.experimental import pallas as pl
from jax.experimental.pallas import tpu as pltpu

F32 = jnp.float32
BF16 = jnp.bfloat16
HIGHEST = lax.Precision.HIGHEST

DIFF_HEADS = 8
DIFF_HEAD_DIM = 64
RET_HEADS = 8
RET_QK_DIM = 128
LANE = 128
N_EXPERTS = 16
EC_CAPACITY_FACTOR = 2
ROPE_THETA = 10000.0
NORM_EPS = 1e-6
RET_CHUNK = 256
VMEM_LIMIT = 56 * 1024 * 1024


def _cparams(sem):
    return pltpu.CompilerParams(dimension_semantics=sem, vmem_limit_bytes=VMEM_LIMIT)


def _silu(x):
    return x / (1.0 + jnp.exp(-x))


def _rms(x, eps=NORM_EPS):
    return x * lax.rsqrt(jnp.mean(x * x, axis=-1, keepdims=True) + eps)


def _ada_kernel(c_ref, w_ref, b_ref, o_ref):
    s = _silu(c_ref[...])
    o_ref[...] = jnp.dot(s, w_ref[...], precision=HIGHEST, preferred_element_type=F32) + b_ref[...]


def _ada(c_pad, w_ada, b_ada):
    m, d = c_pad.shape
    n = w_ada.shape[1]
    tn = min(d, 1024)
    assert n % tn == 0
    return pl.pallas_call(
        _ada_kernel,
        grid=(n // tn,),
        in_specs=[pl.BlockSpec((m, d), lambda j: (0, 0)),
                  pl.BlockSpec((d, tn), lambda j: (0, j)),
                  pl.BlockSpec((1, tn), lambda j: (0, j))],
        out_specs=pl.BlockSpec((m, tn), lambda j: (0, j)),
        out_shape=jax.ShapeDtypeStruct((m, n), F32),
        compiler_params=_cparams(("arbitrary",)),
        name="ada",
    )(c_pad, w_ada, b_ada)


def _rope_tab_kernel(pos_ref, cd_ref, sda_ref, sdb_ref, cr_ref, sr_ref):
    pos = pos_ref[...].astype(F32)
    lane = lax.broadcasted_iota(jnp.int32, (pos.shape[0], LANE), 1)
    is_d = lane < 32
    expo = jnp.where(is_d, lane.astype(F32) * (2.0 / DIFF_HEAD_DIM),
                     (lane - 32).astype(F32) * (2.0 / RET_QK_DIM))
    inv = jnp.exp(expo * (-math.log(ROPE_THETA)))
    ang = pos * inv
    c = jnp.cos(ang)
    s = jnp.sin(ang)
    c32, c64, c96 = pltpu.roll(c, 32, 1), pltpu.roll(c, 64, 1), pltpu.roll(c, 96, 1)
    s32, s64, s96 = pltpu.roll(s, 32, 1), pltpu.roll(s, 64, 1), pltpu.roll(s, 96, 1)
    q = lane // 32
    cos_d = jnp.where(q == 0, c, jnp.where(q == 1, c32, jnp.where(q == 2, c64, c96)))
    sin_d = jnp.where(q == 0, s, jnp.where(q == 1, s32, jnp.where(q == 2, s64, s96)))
    first_half = (q == 0) | (q == 2)
    cd_ref[...] = cos_d
    sda_ref[...] = jnp.where(first_half, -sin_d, 0.0)
    sdb_ref[...] = jnp.where(first_half, 0.0, sin_d)
    lo = lane < 64
    cr_ref[...] = jnp.where(lo, c96, c32)
    sr_ref[...] = jnp.where(lo, -s96, s32)


def _rope_tables(pos_col):
    t = pos_col.shape[0]
    tm = min(t, 2048)
    spec = pl.BlockSpec((tm, LANE), lambda i: (i, 0))
    return pl.pallas_call(
        _rope_tab_kernel,
        grid=(t // tm,),
        in_specs=[pl.BlockSpec((tm, 1), lambda i: (i, 0))],
        out_specs=[spec] * 5,
        out_shape=[jax.ShapeDtypeStruct((t, LANE), F32)] * 5,
        compiler_params=_cparams(("arbitrary",)),
        name="rope_tab",
    )(pos_col)


def _inproj_kernel(x_ref, sc_ref, sh_ref, g_ref, w_ref, cd_ref, sda_ref, sdb_ref, cr_ref, sr_ref,
                   o_ref, og_ref, h_ref, acc_ref, *, n_col, q_scale, rk_scale):
    j = pl.program_id(1)
    nslab = o_ref.shape[1] // LANE

    def epilogue(kind, slot):
        if kind in (0, 1):
            scale = q_scale if kind == 0 else 1.0
            cd, sa, sb = cd_ref[...], sda_ref[...], sdb_ref[...]
            for s in range(nslab):
                xs = acc_ref[slot, :, s * LANE:(s + 1) * LANE]
                y = xs * cd + pltpu.roll(xs, 96, 1) * sa + pltpu.roll(xs, 32, 1) * sb
                o_ref[:, s * LANE:(s + 1) * LANE] = (y * scale).astype(BF16)
        elif kind in (3, 4):
            scale = rk_scale if kind == 4 else 1.0
            cr, sr = cr_ref[...], sr_ref[...]
            for s in range(nslab):
                xs = acc_ref[slot, :, s * LANE:(s + 1) * LANE]
                y = xs * cr + pltpu.roll(xs, 64, 1) * sr
                o_ref[:, s * LANE:(s + 1) * LANE] = (y * scale).astype(BF16)
        else:
            o_ref[...] = acc_ref[slot].astype(BF16)

    for jj in range(n_col):
        @pl.when(j == jj)
        def _(jj=jj):
            if jj == 0:
                h = _rms(x_ref[...]) * g_ref[...] * (1.0 + sc_ref[0]) + sh_ref[0]
                h_ref[...] = h.astype(BF16)
            acc_ref[jj % 2] = jnp.dot(h_ref[...], w_ref[...], preferred_element_type=F32)
            if jj > 0:
                epilogue(jj - 1, (jj - 1) % 2)
            if jj == n_col - 1:
                og_ref[...] = acc_ref[jj % 2].astype(BF16)


def _inproj(x2, scale1, shift1, g, w_bf, tabs, seq):
    t, d = x2.shape
    nc = w_bf.shape[1]
    n_col = 7
    tn = nc // n_col
    tm = min(seq, 1024)
    tiles_per_seq = seq // tm
    kern = functools.partial(
        _inproj_kernel, n_col=n_col,
        q_scale=(DIFF_HEAD_DIM ** -0.5) * math.log2(math.e), rk_scale=RET_QK_DIM ** -0.5)
    mod_spec = pl.BlockSpec((1, 1, d), lambda i, j: (i // tiles_per_seq, 0, 0))
    tab_spec = pl.BlockSpec((tm, LANE), lambda i, j: (i, 0))
    return pl.pallas_call(
        kern,
        grid=(t // tm, n_col),
        in_specs=[pl.BlockSpec((tm, d), lambda i, j: (i, 0)), mod_spec, mod_spec,
                  pl.BlockSpec((1, d), lambda i, j: (0, 0)),
                  pl.BlockSpec((d, tn), lambda i, j: (0, j))] + [tab_spec] * 5,
        out_specs=[pl.BlockSpec((tm, tn), lambda i, j: (i, jnp.maximum(j - 1, 0))),
                   pl.BlockSpec((tm, tn), lambda i, j: (i, 0))],
        out_shape=[jax.ShapeDtypeStruct((t, nc - tn), BF16), jax.ShapeDtypeStruct((t, tn), BF16)],
        scratch_shapes=[pltpu.VMEM((tm, d), BF16), pltpu.VMEM((2, tm, tn), F32)],
        compiler_params=_cparams(("arbitrary", "arbitrary")),
        name="inproj",
    )(x2, scale1, shift1, g, w_bf, *tabs)


def _diffattn_kernel(lq1_ref, lk1_ref, lq2_ref, lk2_ref, q_ref, k_ref, v_ref, g_ref, o_ref,
                     vt_ref, s_ref, mx_ref, out0_ref, *, lam_init, tk):
    i = pl.program_id(2)
    last = pl.num_programs(2) - 1
    dv = v_ref.shape[2]
    tq = q_ref.shape[1]
    s_len = k_ref.shape[1]
    nk = s_len // tk

    @pl.when(i == 0)
    def _():
        extra = lax.broadcasted_iota(jnp.int32, (vt_ref.shape[0] - dv, s_len), 0)
        vt_ref[0:dv, :] = v_ref[0].astype(F32).T.astype(BF16)
        vt_ref[dv:, :] = jnp.where(extra == 0, 1.0, 0.0).astype(BF16)
        s_ref[1] = jnp.zeros(s_ref.shape[1:], F32)
        mx_ref[1] = jnp.zeros(mx_ref.shape[1:], F32)

    q = q_ref[0]
    lane = lax.broadcasted_iota(jnp.int32, q.shape, 1)

    def phase(score_map, value_map):
        lo = score_map * DIFF_HEAD_DIM
        qm = jnp.where((lane >= lo) & (lane < lo + DIFF_HEAD_DIM), q, jnp.zeros_like(q))
        mx = jnp.max(mx_ref[value_map], axis=0, keepdims=True)
        cmax = jnp.full((8, tq), -jnp.inf, F32)
        acc = jnp.zeros((vt_ref.shape[0], tq), F32)
        for c in range(nk):
            rows = slice(c * tk, (c + 1) * tk)
            sc = lax.dot_general(k_ref[0, rows, :], qm, (((1,), (1,)), ((), ())),
                                 preferred_element_type=F32)
            s_ref[score_map, rows, :] = sc
            cmax = jnp.maximum(cmax, jnp.max(sc.reshape(tk // 8, 8, tq), axis=0))
            p = jnp.exp2((s_ref[value_map, rows, :] - mx).astype(BF16))
            acc = acc + jnp.dot(vt_ref[:, rows], p, preferred_element_type=F32)
        mx_ref[score_map] = cmax
        return acc[0:dv, :] / acc[dv:dv + 1, :]

    res1 = phase(0, 1)

    @pl.when(i > 0)
    def _():
        lam = (jnp.exp(jnp.sum(lq1_ref[...] * lk1_ref[...], axis=-1, keepdims=True))
               - jnp.exp(jnp.sum(lq2_ref[...] * lk2_ref[...], axis=-1, keepdims=True)) + lam_init)
        ad = out0_ref[...] - lam * res1
        y = ad * lax.rsqrt(jnp.mean(ad * ad, axis=0, keepdims=True) + NORM_EPS)
        o_ref[0] = (y.T * (g_ref[...] * (1.0 - lam_init))).astype(BF16)

    @pl.when(i < last)
    def _():
        out0_ref[...] = phase(1, 0)


def _diffattn(proj3, lams, subln_gain, lam_init):
    b, s, _ = proj3.shape
    h = DIFF_HEADS
    tq = min(s, 512)
    tk = min(s, 512)
    nq = s // tq
    kern = functools.partial(_diffattn_kernel, lam_init=lam_init, tk=tk)
    lam_spec = pl.BlockSpec((1, DIFF_HEAD_DIM), lambda bi, hi, i: (0, 0))
    return pl.pallas_call(
        kern,
        grid=(b, h, nq + 1),
        in_specs=[lam_spec] * 4 + [
            pl.BlockSpec((1, tq, LANE), lambda bi, hi, i: (bi, jnp.minimum(i, nq - 1), hi)),
            pl.BlockSpec((1, s, LANE), lambda bi, hi, i: (bi, 0, h + hi)),
            pl.BlockSpec((1, s, LANE), lambda bi, hi, i: (bi, 0, 2 * h + hi)),
            pl.BlockSpec((1, LANE), lambda bi, hi, i: (0, 0))],
        out_specs=pl.BlockSpec((1, tq, LANE), lambda bi, hi, i: (bi, jnp.maximum(i - 1, 0), hi)),
        out_shape=jax.ShapeDtypeStruct((b, s, h * LANE), BF16),
        scratch_shapes=[pltpu.VMEM((LANE + 16, s), BF16), pltpu.VMEM((2, s, tq), F32),
                        pltpu.VMEM((2, 8, tq), F32), pltpu.VMEM((LANE, tq), F32)],
        compiler_params=_cparams(("arbitrary", "arbitrary", "arbitrary")),
        name="diffattn",
    )(*lams, proj3, proj3, proj3, subln_gain)


def _retention_kernel(logit_ref, q_ref, k_ref, v_ref, gate_ref, gain_ref, o_ref, acc_ref, accb_ref, *, chunk):
    hi = pl.program_id(1)
    c_len = chunk
    s_len = q_ref.shape[1]
    n_chunks = s_len // c_len
    lane = lax.broadcasted_iota(jnp.int32, (1, logit_ref.shape[1]), 1)
    lg = logit_ref[...]
    log_g = -jnp.log(1.0 + jnp.exp(-lg))
    log_g = jnp.sum(jnp.where(lane == hi, log_g, 0.0), axis=-1, keepdims=True)
    lgf, lgb = log_g[0:1, :], log_g[1:2, :]

    ii = lax.broadcasted_iota(jnp.int32, (c_len, c_len), 0).astype(F32)
    jj = lax.broadcasted_iota(jnp.int32, (c_len, c_len), 1).astype(F32)
    dist = ii - jj
    decay = jnp.where(dist >= 0, jnp.exp(lgf * jnp.maximum(dist, 0.0)),
                      jnp.exp(lgb * jnp.maximum(-dist, 0.0)))
    ci = lax.broadcasted_iota(jnp.int32, (c_len, 1), 0).astype(F32)
    qdec_f = jnp.exp(lgf * (ci + 1.0))
    kdec_f = jnp.exp(lgf * (c_len - 1.0 - ci))
    qdec_b = jnp.exp(lgb * (c_len - ci))
    kdec_b = jnp.exp(lgb * ci)
    cdec_f = jnp.exp(lgf * c_len)
    cdec_b = jnp.exp(lgb * c_len)
    dk = q_ref.shape[2]
    dv = v_ref.shape[2]

    def load(c):
        r = pl.ds(pl.multiple_of(c * c_len, c_len), c_len)
        return q_ref[0, r, :], k_ref[0, r, :], v_ref[0, r, :], r

    def fwd(c, state):
        q, k, v, r = load(c)
        qf, kf = q.astype(F32), k.astype(F32)
        sc = lax.dot_general(q, k, (((1,), (1,)), ((), ())), preferred_element_type=F32) * decay
        inner = jnp.dot(sc.astype(BF16), v, preferred_element_type=F32)
        cross = jnp.dot((qf * qdec_f).astype(BF16), state.astype(BF16), preferred_element_type=F32)
        acc_ref[r, :] = inner + cross
        kv = lax.dot_general((kf * kdec_f).astype(BF16), v, (((0,), (0,)), ((), ())),
                             preferred_element_type=F32)
        return state * cdec_f + kv

    def bwd(c, state):
        q, k, v, r = load(c)
        qf, kf = q.astype(F32), k.astype(F32)
        accb_ref[r, :] = jnp.dot((qf * qdec_b).astype(BF16), state.astype(BF16), preferred_element_type=F32)
        kv = lax.dot_general((kf * kdec_b).astype(BF16), v, (((0,), (0,)), ((), ())),
                             preferred_element_type=F32)
        return state * cdec_b + kv

    def both(i, states):
        return fwd(i, states[0]), bwd(n_chunks - 1 - i, states[1])

    zero = jnp.zeros((dk, dv), F32)
    lax.fori_loop(0, n_chunks, both, (zero, zero), unroll=2)

    ro = acc_ref[...] + accb_ref[...]
    mu = jnp.mean(ro, axis=-1, keepdims=True)
    xc = ro - mu
    y = xc * lax.rsqrt(jnp.mean(xc * xc, axis=-1, keepdims=True) + NORM_EPS) * gain_ref[...]
    o_ref[0] = (_silu(gate_ref[0].astype(F32)) * y).astype(BF16)


def _retention(proj3, gate3, decay_logit, norm_gain):
    b, s, _ = proj3.shape
    h = RET_HEADS
    base = 3 * DIFF_HEADS
    kern = functools.partial(_retention_kernel, chunk=min(RET_CHUNK, s))

    def col(off):
        return pl.BlockSpec((1, s, LANE), lambda bi, hi: (bi, 0, base + off * h + hi))

    return pl.pallas_call(
        kern,
        grid=(b, h),
        in_specs=[pl.BlockSpec((2, h), lambda bi, hi: (0, 0)), col(0), col(1), col(2),
                  pl.BlockSpec((1, s, LANE), lambda bi, hi: (bi, 0, hi)),
                  pl.BlockSpec((1, LANE), lambda bi, hi: (0, hi))],
        out_specs=pl.BlockSpec((1, s, LANE), lambda bi, hi: (bi, 0, hi)),
        out_shape=jax.ShapeDtypeStruct((b, s, h * LANE), BF16),
        scratch_shapes=[pltpu.VMEM((s, LANE), F32), pltpu.VMEM((s, LANE), F32)],
        compiler_params=_cparams(("arbitrary", "arbitrary")),
        name="retention",
    )(decay_logit, proj3, proj3, proj3, gate3, norm_gain)


def _outproj_kernel(ad_ref, ro_ref, wa_ref, wr_ref, x_ref, g1_ref, sc2_ref, sh2_ref, gpost_ref, gpre_ref,
                    wrt_ref, x1_ref, h2_ref, lt_ref):
    mix = (jnp.dot(ad_ref[...], wa_ref[...], preferred_element_type=F32)
           + jnp.dot(ro_ref[...], wr_ref[...], preferred_element_type=F32))
    x1 = x_ref[...] + g1_ref[0] * (_rms(mix) * gpost_ref[...])
    x1_ref[...] = x1
    h2 = _rms(x1) * gpre_ref[...] * (1.0 + sc2_ref[0]) + sh2_ref[0]
    h2_ref[...] = h2
    h_hi = h2.astype(BF16)
    h_lo = (h2 - h_hi.astype(F32)).astype(BF16)
    w = wrt_ref[...]
    n_e = w.shape[0]
    w_hi = w.astype(BF16)
    w_lo = (w - w_hi.astype(F32)).astype(BF16)
    nt = (((1,), (1,)), ((), ()))
    r1 = lax.dot_general(jnp.concatenate([w_hi, w_lo], axis=0), h_hi, nt, preferred_element_type=F32)
    r2 = lax.dot_general(w_hi, h_lo, nt, preferred_element_type=F32)
    lt_ref[...] = r1[0:n_e, :] + r1[n_e:, :] + r2


def _outproj(ad2, ro2, wa_bf, wr_bf, x2, gate1, scale2, shift2, gpost, gpre, w_router_t, seq):
    t, d = x2.shape
    ka, kr = ad2.shape[1], ro2.shape[1]
    e = w_router_t.shape[0]
    tm = min(seq, 512)
    tiles_per_seq = seq // tm
    mod_spec = pl.BlockSpec((1, 1, d), lambda i: (i // tiles_per_seq, 0, 0))
    vec_spec = pl.BlockSpec((1, d), lambda i: (0, 0))
    row_spec = pl.BlockSpec((tm, d), lambda i: (i, 0))
    return pl.pallas_call(
        _outproj_kernel,
        grid=(t // tm,),
        in_specs=[pl.BlockSpec((tm, ka), lambda i: (i, 0)), pl.BlockSpec((tm, kr), lambda i: (i, 0)),
                  pl.BlockSpec((ka, d), lambda i: (0, 0)), pl.BlockSpec((kr, d), lambda i: (0, 0)),
                  row_spec, mod_spec, mod_spec, mod_spec, vec_spec, vec_spec,
                  pl.BlockSpec((e, d), lambda i: (0, 0))],
        out_specs=[row_spec, row_spec, pl.BlockSpec((e, tm), lambda i: (0, i))],
        out_shape=[jax.ShapeDtypeStruct((t, d), F32), jax.ShapeDtypeStruct((t, d), F32),
                   jax.ShapeDtypeStruct((e, t), F32)],
        compiler_params=_cparams(("arbitrary",)),
        name="outproj",
    )(ad2, ro2, wa_bf, wr_bf, x2, gate1, scale2, shift2, gpost, gpre, w_router_t)


def _lane_cumsum_exclusive(x, tri):
    r, s = x.shape
    carry = jnp.zeros((r, 1), F32)
    cols = []
    for blk in range(s // LANE):
        xb = x[:, blk * LANE:(blk + 1) * LANE]
        inc = jnp.dot(xb.astype(BF16), tri, preferred_element_type=F32)
        cols.append(inc - xb + carry)
        carry = carry + inc[:, LANE - 1:LANE]
    return jnp.concatenate(cols, axis=1)


def _route_kernel(lt_ref, post_ref, info_ref, *, cap):
    logits = lt_ref[...]
    n_e, s_len = logits.shape
    mx = jnp.max(logits, axis=0, keepdims=True)
    ex = jnp.exp(logits - mx)
    aff = ex / jnp.sum(ex, axis=0, keepdims=True)

    kf = float(cap)
    tbits = jnp.zeros((n_e, 1), jnp.int32)
    for bit in range(30, -1, -1):
        cand = tbits | (1 << bit)
        cnt = jnp.sum(jnp.where(aff >= pltpu.bitcast(cand, F32), 1.0, 0.0), axis=1, keepdims=True)
        tbits = jnp.where(cnt >= kf, cand, tbits)
    thr = pltpu.bitcast(tbits, F32)

    ri = lax.broadcasted_iota(jnp.int32, (LANE, LANE), 0)
    rj = lax.broadcasted_iota(jnp.int32, (LANE, LANE), 1)
    tri = jnp.where(ri <= rj, 1.0, 0.0).astype(BF16)
    gt = jnp.where(aff > thr, 1.0, 0.0)
    tie = jnp.where(aff == thr, 1.0, 0.0)
    need = kf - jnp.sum(gt, axis=1, keepdims=True)
    tie_rank = _lane_cumsum_exclusive(tie, tri)
    sel = gt + tie * jnp.where(tie_rank < need, 1.0, 0.0)
    pos = _lane_cumsum_exclusive(sel, tri)
    pos = jnp.where(sel > 0.5, pos, -1.0)

    tok_major = jnp.concatenate([pos, jnp.zeros((LANE - n_e, s_len), F32)], axis=0).T
    post_ref[0] = tok_major.astype(jnp.int32)
    a1 = aff.astype(BF16)
    r1 = aff - a1.astype(F32)
    a2 = r1.astype(BF16)
    a3 = (r1 - a2.astype(F32)).astype(BF16)
    tok = lax.broadcasted_iota(jnp.int32, (n_e, s_len), 1)
    row = lax.broadcasted_iota(jnp.int32, (n_e, s_len), 0)
    digits = jnp.where(row == 0, tok // 64, jnp.where(row == 1, tok % 64, 0)).astype(F32).astype(BF16)
    rows = jnp.concatenate([digits, a1, a2, a3, jnp.zeros((LANE - 4 * n_e, s_len), BF16)], axis=0)
    slot = lax.broadcasted_iota(jnp.int32, (s_len, cap), 1).astype(F32)
    for e in range(n_e):
        onehot = jnp.where(tok_major[:, e:e + 1] == slot, 1.0, 0.0).astype(BF16)
        res = jnp.dot(rows, onehot, preferred_element_type=F32)
        tok_idx = res[0:1, :] * 64.0 + res[1:2, :]
        gate = (res[n_e + e:n_e + e + 1, :] + res[2 * n_e + e:2 * n_e + e + 1, :]
                + res[3 * n_e + e:3 * n_e + e + 1, :])
        info_ref[0, e] = jnp.concatenate([tok_idx, gate, jnp.zeros((6, cap), F32)], axis=0)


def _route(logits_t, b, s, cap):
    n_e = logits_t.shape[0]
    kern = functools.partial(_route_kernel, cap=cap)
    return pl.pallas_call(
        kern,
        grid=(b,),
        in_specs=[pl.BlockSpec((n_e, s), lambda bi: (0, bi))],
        out_specs=[pl.BlockSpec((1, s, LANE), lambda bi: (bi, 0, 0)),
                   pl.BlockSpec((1, n_e, 8, cap), lambda bi: (bi, 0, 0, 0))],
        out_shape=[jax.ShapeDtypeStruct((b, s, LANE), jnp.int32),
                   jax.ShapeDtypeStruct((b, n_e, 8, cap), F32)],
        compiler_params=_cparams(("arbitrary",)),
        name="route",
    )(logits_t)


def _gather_kernel(rows_ref, src_ref, o_ref, buf_ref, sem, *, rows_per_step):
    i = pl.program_id(0)

    def issue(step, slot):
        base = step * rows_per_step

        def body(k, carry):
            pltpu.make_async_copy(src_ref.at[pl.ds(rows_ref[base + k], 1)],
                                  buf_ref.at[slot, pl.ds(k, 1)], sem.at[slot]).start()
            return carry

        lax.fori_loop(0, rows_per_step, body, 0, unroll=8)

    @pl.when(i == 0)
    def _():
        issue(0, 0)

    @pl.when(i + 1 < pl.num_programs(0))
    def _():
        issue(i + 1, (i + 1) % 2)

    slot = i % 2
    pltpu.make_async_copy(src_ref.at[pl.ds(0, rows_per_step)], buf_ref.at[slot], sem.at[slot]).wait()
    o_ref[...] = buf_ref[slot].astype(o_ref.dtype)


def _gather_rows(rows, src):
    n = rows.shape[0]
    d = src.shape[1]
    rows_per_step = min(n, 512)
    kern = functools.partial(_gather_kernel, rows_per_step=rows_per_step)
    return pl.pallas_call(
        kern,
        grid_spec=pltpu.PrefetchScalarGridSpec(
            num_scalar_prefetch=1,
            grid=(n // rows_per_step,),
            in_specs=[pl.BlockSpec(memory_space=pl.ANY)],
            out_specs=pl.BlockSpec((rows_per_step, d), lambda i, rows: (i, 0)),
            scratch_shapes=[pltpu.VMEM((2, rows_per_step, d), src.dtype), pltpu.SemaphoreType.DMA((2,))]),
        out_shape=jax.ShapeDtypeStruct((n, d), BF16),
        compiler_params=_cparams(("arbitrary",)),
        name="gather",
    )(rows, src)


def _ffn_kernel(x_ref, gate_ref, wg_ref, wu_ref, wd_ref, y_ref, acc_ref):
    f = pl.program_id(2)

    @pl.when(f == 0)
    def _():
        acc_ref[...] = jnp.zeros_like(acc_ref)

    xb = x_ref[...]
    a = jnp.dot(xb, wg_ref[0].astype(BF16), preferred_element_type=F32)
    u = jnp.dot(xb, wu_ref[0].astype(BF16), preferred_element_type=F32)
    hmid = (_silu(a) * u).astype(BF16)
    acc_ref[...] += jnp.dot(hmid, wd_ref[0].astype(BF16), preferred_element_type=F32)

    @pl.when(f == pl.num_programs(2) - 1)
    def _():
        y_ref[...] = (acc_ref[...] * gate_ref[...]).astype(BF16)


def _ffn(xg, gate_col, w_gate, w_up, w_down, rows_per_expert):
    n, d = xg.shape
    n_e, _, ff = w_gate.shape
    tm = min(rows_per_expert, 1024)
    tf = min(ff, 512)
    tiles_per_expert = rows_per_expert // tm
    row_map = lambda e, r, f: (e * tiles_per_expert + r, 0)
    return pl.pallas_call(
        _ffn_kernel,
        grid=(n_e, tiles_per_expert, ff // tf),
        in_specs=[pl.BlockSpec((tm, d), row_map), pl.BlockSpec((tm, 1), row_map),
                  pl.BlockSpec((1, d, tf), lambda e, r, f: (e, 0, f)),
                  pl.BlockSpec((1, d, tf), lambda e, r, f: (e, 0, f)),
                  pl.BlockSpec((1, tf, d), lambda e, r, f: (e, f, 0))],
        out_specs=pl.BlockSpec((tm, d), row_map),
        out_shape=jax.ShapeDtypeStruct((n, d), BF16),
        scratch_shapes=[pltpu.VMEM((tm, d), F32)],
        compiler_params=_cparams(("arbitrary", "arbitrary", "arbitrary")),
        name="ffn",
    )(xg, gate_col, w_gate, w_up, w_down)


def _combine_kernel(post_ref, *refs, experts_per_step):
    y_refs = refs[:experts_per_step]
    x1_ref, g2_ref, gpost_ref, o_ref, acc_ref = refs[experts_per_step:]
    eg = pl.program_id(2)

    @pl.when(eg == 0)
    def _():
        acc_ref[...] = jnp.zeros_like(acc_ref)

    posmat = post_ref[0]
    lane = lax.broadcasted_iota(jnp.int32, posmat.shape, 1)
    cap = y_refs[0].shape[0]
    kw = min(cap, 256)
    total = None
    spill = []
    for k, y_ref in enumerate(y_refs):
        e = eg * experts_per_step + k
        pos = jnp.sum(jnp.where(lane == e, posmat, 0), axis=1, keepdims=True)
        hi = jnp.max(pos)
        lo = jnp.min(jnp.where(pos >= 0, pos, cap))
        start = pl.multiple_of(jnp.minimum((lo // 16) * 16, cap - kw), 16)
        slot = lax.broadcasted_iota(jnp.int32, (pos.shape[0], kw), 1) + start
        onehot = jnp.where(pos == slot, 1.0, 0.0).astype(BF16)
        part = jnp.dot(onehot, y_ref[pl.ds(start, kw), :], preferred_element_type=F32)
        total = part if total is None else total + part
        spill.append((pos, hi, start, y_ref))
    acc_ref[...] += total

    for pos, hi, start, y_ref in spill:
        @pl.when(hi >= start + kw)
        def _(pos=pos, start=start, y_ref=y_ref):
            slot = lax.broadcasted_iota(jnp.int32, (pos.shape[0], cap), 1)
            onehot = jnp.where((pos == slot) & (slot >= start + kw), 1.0, 0.0).astype(BF16)
            acc_ref[...] += jnp.dot(onehot, y_ref[...], preferred_element_type=F32)

    @pl.when(eg == pl.num_programs(2) - 1)
    def _():
        o_ref[...] = x1_ref[...] + g2_ref[0] * (_rms(acc_ref[...]) * gpost_ref[...])


def _combine(pos_t, y, x1, gate2, gpost, b, s, cap):
    t, d = x1.shape
    n_e = y.shape[0] // (b * cap)
    tt = min(s, 512)
    tiles_per_seq = s // tt
    eps = 4
    assert n_e % eps == 0
    y_specs = [pl.BlockSpec((cap, d), lambda bi, ji, eg, k=k: ((eg * eps + k) * b + bi, 0)) for k in range(eps)]
    return pl.pallas_call(
        functools.partial(_combine_kernel, experts_per_step=eps),
        grid=(b, tiles_per_seq, n_e // eps),
        in_specs=[pl.BlockSpec((1, tt, LANE), lambda bi, ji, eg: (bi, ji, 0))] + y_specs + [
            pl.BlockSpec((tt, d), lambda bi, ji, eg: (bi * tiles_per_seq + ji, 0)),
            pl.BlockSpec((1, 1, d), lambda bi, ji, eg: (bi, 0, 0)),
            pl.BlockSpec((1, d), lambda bi, ji, eg: (0, 0))],
        out_specs=pl.BlockSpec((tt, d), lambda bi, ji, eg: (bi * tiles_per_seq + ji, 0)),
        out_shape=jax.ShapeDtypeStruct((t, d), F32),
        scratch_shapes=[pltpu.VMEM((tt, d), F32)],
        compiler_params=_cparams(("arbitrary", "arbitrary", "arbitrary")),
        name="combine",
    )(pos_t, *([y] * eps), x1, gate2, gpost)


def kernel(x, c, positions, w_ada, b_ada, g_pre_mix, g_post_mix, w_in, diff_lambda_q1, diff_lambda_k1,
           diff_lambda_q2, diff_lambda_k2, diff_subln_gain, ret_decay_logit, ret_norm_gain, w_out, g_pre_ffn,
           g_post_ffn, w_router, w_gate, w_up, w_down):
    b, s, d = x.shape
    t = b * s
    depth = w_ada.shape[0]
    n_e = w_router.shape[2]
    cap = EC_CAPACITY_FACTOR * s // n_e
    diff_w = DIFF_HEADS * LANE

    pos_col = positions.reshape(t, 1)
    tabs = _rope_tables(pos_col)
    c_pad = jnp.pad(c, ((0, (-b) % 8), (0, 0)))
    x2 = x.reshape(t, d)
    for l in range(depth):
        lam_init = 0.8 - 0.6 * math.exp(-0.3 * l)
        mod = _ada(c_pad, w_ada[l], b_ada[l].reshape(1, -1))[:b]
        shift1, scale1, gate1, shift2, scale2, gate2 = [m.reshape(b, 1, d) for m in jnp.split(mod, 6, axis=-1)]

        proj, rgate = _inproj(x2, scale1, shift1, g_pre_mix[l].reshape(1, d), w_in[l].astype(BF16), tabs, s)
        proj3 = proj.reshape(b, s, -1)
        lams = [v[l].reshape(1, -1) for v in (diff_lambda_q1, diff_lambda_k1, diff_lambda_q2, diff_lambda_k2)]
        ad = _diffattn(proj3, lams, diff_subln_gain[l].reshape(1, -1), lam_init)
        ro = _retention(proj3, rgate.reshape(b, s, -1), ret_decay_logit[l], ret_norm_gain[l].reshape(1, -1))

        w_out_bf = w_out[l].astype(BF16)
        x1, h2, logits_t = _outproj(
            ad.reshape(t, -1), ro.reshape(t, -1), w_out_bf[:diff_w], w_out_bf[diff_w:], x2, gate1, scale2,
            shift2, g_post_mix[l].reshape(1, d), g_pre_ffn[l].reshape(1, d), w_router[l].T, s)

        pos_t, info = _route(logits_t, b, s, cap)
        rows = info[:, :, 0, :].astype(jnp.int32) + (jnp.arange(b, dtype=jnp.int32) * s)[:, None, None]
        rows = jnp.transpose(rows, (1, 0, 2)).reshape(-1)
        gate_col = jnp.transpose(info[:, :, 1, :], (1, 0, 2)).reshape(-1, 1)
        xg = _gather_rows(rows, h2)
        y = _ffn(xg, gate_col, w_gate[l], w_up[l], w_down[l], b * cap)
        x2 = _combine(pos_t, y, x1, gate2, g_post_ffn[l].reshape(1, d), b, s, cap)
    return x2.reshape(b, s, d)
```

```python
import functools
import math

import jax
import jax.numpy as jnp
from jax import lax
from jax.experimental import pallas as pl
from jax.experimental.pallas import tpu as pltpu

F32 = jnp.float32
BF16 = jnp.bfloat16
HIGHEST = lax.Precision.HIGHEST

DIFF_HEADS = 8
DIFF_HEAD_DIM = 64
RET_HEADS = 8
RET_QK_DIM = 128
LANE = 128
N_EXPERTS = 16
EC_CAPACITY_FACTOR = 2
ROPE_THETA = 10000.0
NORM_EPS = 1e-6
RET_CHUNK = 256
VMEM_LIMIT = 56 * 1024 * 1024


def _cparams(sem):
    return pltpu.CompilerParams(dimension_semantics=sem, vmem_limit_bytes=VMEM_LIMIT)


def _silu(x):
    return x / (1.0 + jnp.exp(-x))


def _rms(x, eps=NORM_EPS):
    return x * lax.rsqrt(jnp.mean(x * x, axis=-1, keepdims=True) + eps)


def _ada_kernel(c_ref, w_ref, b_ref, o_ref):
    s = _silu(c_ref[...])
    o_ref[...] = jnp.dot(s, w_ref[...], precision=HIGHEST, preferred_element_type=F32) + b_ref[...]


def _ada(c_pad, w_ada, b_ada):
    m, d = c_pad.shape
    n = w_ada.shape[1]
    tn = min(d, 1024)
    assert n % tn == 0
    return pl.pallas_call(
        _ada_kernel,
        grid=(n // tn,),
        in_specs=[pl.BlockSpec((m, d), lambda j: (0, 0)),
                  pl.BlockSpec((d, tn), lambda j: (0, j)),
                  pl.BlockSpec((1, tn), lambda j: (0, j))],
        out_specs=pl.BlockSpec((m, tn), lambda j: (0, j)),
        out_shape=jax.ShapeDtypeStruct((m, n), F32),
        compiler_params=_cparams(("arbitrary",)),
        name="ada",
    )(c_pad, w_ada, b_ada)


def _rope_tab_kernel(pos_ref, cd_ref, sda_ref, sdb_ref, cr_ref, sr_ref):
    pos = pos_ref[...].astype(F32)
    lane = lax.broadcasted_iota(jnp.int32, (pos.shape[0], LANE), 1)
    is_d = lane < 32
    expo = jnp.where(is_d, lane.astype(F32) * (2.0 / DIFF_HEAD_DIM),
                     (lane - 32).astype(F32) * (2.0 / RET_QK_DIM))
    inv = jnp.exp(expo * (-math.log(ROPE_THETA)))
    ang = pos * inv
    c = jnp.cos(ang)
    s = jnp.sin(ang)
    c32, c64, c96 = pltpu.roll(c, 32, 1), pltpu.roll(c, 64, 1), pltpu.roll(c, 96, 1)
    s32, s64, s96 = pltpu.roll(s, 32, 1), pltpu.roll(s, 64, 1), pltpu.roll(s, 96, 1)
    q = lane // 32
    cos_d = jnp.where(q == 0, c, jnp.where(q == 1, c32, jnp.where(q == 2, c64, c96)))
    sin_d = jnp.where(q == 0, s, jnp.where(q == 1, s32, jnp.where(q == 2, s64, s96)))
    first_half = (q == 0) | (q == 2)
    cd_ref[...] = cos_d
    sda_ref[...] = jnp.where(first_half, -sin_d, 0.0)
    sdb_ref[...] = jnp.where(first_half, 0.0, sin_d)
    lo = lane < 64
    cr_ref[...] = jnp.where(lo, c96, c32)
    sr_ref[...] = jnp.where(lo, -s96, s32)


def _rope_tables(pos_col):
    t = pos_col.shape[0]
    tm = min(t, 2048)
    spec = pl.BlockSpec((tm, LANE), lambda i: (i, 0))
    return pl.pallas_call(
        _rope_tab_kernel,
        grid=(t // tm,),
        in_specs=[pl.BlockSpec((tm, 1), lambda i: (i, 0))],
        out_specs=[spec] * 5,
        out_shape=[jax.ShapeDtypeStruct((t, LANE), F32)] * 5,
        compiler_params=_cparams(("arbitrary",)),
        name="rope_tab",
    )(pos_col)


def _inproj_kernel(x_ref, sc_ref, sh_ref, g_ref, w_ref, cd_ref, sda_ref, sdb_ref, cr_ref, sr_ref,
                   o_ref, og_ref, h_ref, acc_ref, *, n_col, q_scale, rk_scale):
    j = pl.program_id(1)
    nslab = o_ref.shape[1] // LANE

    def epilogue(kind, slot):
        if kind in (0, 1):
            scale = q_scale if kind == 0 else 1.0
            cd, sa, sb = cd_ref[...], sda_ref[...], sdb_ref[...]
            for s in range(nslab):
                xs = acc_ref[slot, :, s * LANE:(s + 1) * LANE]
                y = xs * cd + pltpu.roll(xs, 96, 1) * sa + pltpu.roll(xs, 32, 1) * sb
                o_ref[:, s * LANE:(s + 1) * LANE] = (y * scale).astype(BF16)
        elif kind in (3, 4):
            scale = rk_scale if kind == 4 else 1.0
            cr, sr = cr_ref[...], sr_ref[...]
            for s in range(nslab):
                xs = acc_ref[slot, :, s * LANE:(s + 1) * LANE]
                y = xs * cr + pltpu.roll(xs, 64, 1) * sr
                o_ref[:, s * LANE:(s + 1) * LANE] = (y * scale).astype(BF16)
        else:
            o_ref[...] = acc_ref[slot].astype(BF16)

    for jj in range(n_col):
        @pl.when(j == jj)
        def _(jj=jj):
            if jj == 0:
                h = _rms(x_ref[...]) * g_ref[...] * (1.0 + sc_ref[0]) + sh_ref[0]
                h_ref[...] = h.astype(BF16)
            acc_ref[jj % 2] = jnp.dot(h_ref[...], w_ref[...], preferred_element_type=F32)
            if jj > 0:
                epilogue(jj - 1, (jj - 1) % 2)
            if jj == n_col - 1:
                og_ref[...] = acc_ref[jj % 2].astype(BF16)


def _inproj(x2, scale1, shift1, g, w_bf, tabs, seq):
    t, d = x2.shape
    nc = w_bf.shape[1]
    n_col = 7
    tn = nc // n_col
    tm = min(seq, 1024)
    tiles_per_seq = seq // tm
    kern = functools.partial(
        _inproj_kernel, n_col=n_col,
        q_scale=(DIFF_HEAD_DIM ** -0.5) * math.log2(math.e), rk_scale=RET_QK_DIM ** -0.5)
    mod_spec = pl.BlockSpec((1, 1, d), lambda i, j: (i // tiles_per_seq, 0, 0))
    tab_spec = pl.BlockSpec((tm, LANE), lambda i, j: (i, 0))
    return pl.pallas_call(
        kern,
        grid=(t // tm, n_col),
        in_specs=[pl.BlockSpec((tm, d), lambda i, j: (i, 0)), mod_spec, mod_spec,
                  pl.BlockSpec((1, d), lambda i, j: (0, 0)),
                  pl.BlockSpec((d, tn), lambda i, j: (0, j))] + [tab_spec] * 5,
        out_specs=[pl.BlockSpec((tm, tn), lambda i, j: (i, jnp.maximum(j - 1, 0))),
                   pl.BlockSpec((tm, tn), lambda i, j: (i, 0))],
        out_shape=[jax.ShapeDtypeStruct((t, nc - tn), BF16), jax.ShapeDtypeStruct((t, tn), BF16)],
        scratch_shapes=[pltpu.VMEM((tm, d), BF16), pltpu.VMEM((2, tm, tn), F32)],
        compiler_params=_cparams(("arbitrary", "arbitrary")),
        name="inproj",
    )(x2, scale1, shift1, g, w_bf, *tabs)


def _diffattn_kernel(lq1_ref, lk1_ref, lq2_ref, lk2_ref, q_ref, k_ref, v_ref, g_ref, o_ref,
                     vt_ref, s_ref, p_ref, mx_ref, l_ref, *, lam_init, tk):
    i = pl.program_id(2)
    nq = pl.num_programs(2) - 2
    dv = v_ref.shape[2]
    tq = q_ref.shape[1]
    s_len = k_ref.shape[1]
    nk = s_len // tk

    @pl.when(i == 0)
    def _():
        vt_ref[...] = v_ref[0].astype(F32).T.astype(BF16)

    def stages(do_score, do_exp, do_value):
        cur = i % 2
        prev = 1 - cur
        if do_score:
            q = q_ref[0]
            lane = lax.broadcasted_iota(jnp.int32, q.shape, 1)
            qm = [jnp.where((lane >= m * DIFF_HEAD_DIM) & (lane < (m + 1) * DIFF_HEAD_DIM), q, jnp.zeros_like(q))
                  for m in range(2)]
            cmax = [jnp.full((8, tq), -jnp.inf, F32) for _ in range(2)]
        if do_exp:
            mx = [jnp.max(mx_ref[prev, m], axis=0, keepdims=True) for m in range(2)]
            lsum = [jnp.zeros((8, tq), F32) for _ in range(2)]
        if do_value:
            lam = (jnp.exp(jnp.sum(lq1_ref[...] * lk1_ref[...], axis=-1, keepdims=True))
                   - jnp.exp(jnp.sum(lq2_ref[...] * lk2_ref[...], axis=-1, keepdims=True)) + lam_init)
            den = [jnp.sum(l_ref[cur, m], axis=0, keepdims=True) for m in range(2)]
            c0 = (1.0 / den[0]).astype(BF16)
            c1 = (lam / den[1]).astype(BF16)
            acc = jnp.zeros((dv, tq), F32)
        for c in range(nk):
            rows = slice(c * tk, (c + 1) * tk)
            if do_value:
                w = p_ref[0, rows, :] * c0 - p_ref[1, rows, :] * c1
                acc = acc + jnp.dot(vt_ref[:, rows], w, preferred_element_type=F32)
            if do_exp:
                for m in range(2):
                    p32 = jnp.exp2(s_ref[m, rows, :] - mx[m])
                    lsum[m] = lsum[m] + jnp.sum(p32.reshape(tk // 8, 8, tq), axis=0)
                    p_ref[m, rows, :] = p32.astype(BF16)
            if do_score:
                for m in range(2):
                    sc = lax.dot_general(k_ref[0, rows, :], qm[m], (((1,), (1,)), ((), ())),
                                         preferred_element_type=F32)
                    s_ref[m, rows, :] = sc
                    cmax[m] = jnp.maximum(cmax[m], jnp.max(sc.reshape(tk // 8, 8, tq), axis=0))
        if do_score:
            for m in range(2):
                mx_ref[cur, m] = cmax[m]
        if do_exp:
            for m in range(2):
                l_ref[prev, m] = lsum[m]
        if do_value:
            y = acc * lax.rsqrt(jnp.mean(acc * acc, axis=0, keepdims=True) + NORM_EPS)
            o_ref[0] = (y.T * (g_ref[...] * (1.0 - lam_init))).astype(BF16)

    pl.when(i == 0)(lambda: stages(True, False, False))
    pl.when((i == 1) & (i < nq))(lambda: stages(True, True, False))
    pl.when((i >= 2) & (i < nq))(lambda: stages(True, True, True))
    pl.when((i == nq) & (i == 1))(lambda: stages(False, True, False))
    pl.when((i == nq) & (i >= 2))(lambda: stages(False, True, True))
    pl.when(i == nq + 1)(lambda: stages(False, False, True))


def _diffattn(proj3, lams, subln_gain, lam_init):
    b, s, _ = proj3.shape
    h = DIFF_HEADS
    tq = min(s, 512)
    tk = min(s, 512)
    nq = s // tq
    kern = functools.partial(_diffattn_kernel, lam_init=lam_init, tk=tk)
    lam_spec = pl.BlockSpec((1, DIFF_HEAD_DIM), lambda bi, hi, i: (0, 0))
    return pl.pallas_call(
        kern,
        grid=(b, h, nq + 2),
        in_specs=[lam_spec] * 4 + [
            pl.BlockSpec((1, tq, LANE), lambda bi, hi, i: (bi, jnp.minimum(i, nq - 1), hi)),
            pl.BlockSpec((1, s, LANE), lambda bi, hi, i: (bi, 0, h + hi)),
            pl.BlockSpec((1, s, LANE), lambda bi, hi, i: (bi, 0, 2 * h + hi)),
            pl.BlockSpec((1, LANE), lambda bi, hi, i: (0, 0))],
        out_specs=pl.BlockSpec((1, tq, LANE), lambda bi, hi, i: (bi, jnp.maximum(i - 2, 0), hi)),
        out_shape=jax.ShapeDtypeStruct((b, s, h * LANE), BF16),
        scratch_shapes=[pltpu.VMEM((LANE, s), BF16), pltpu.VMEM((2, s, tq), F32), pltpu.VMEM((2, s, tq), BF16),
                        pltpu.VMEM((2, 2, 8, tq), F32), pltpu.VMEM((2, 2, 8, tq), F32)],
        compiler_params=_cparams(("arbitrary", "arbitrary", "arbitrary")),
        name="diffattn",
    )(*lams, proj3, proj3, proj3, subln_gain)


def _retention_kernel(logit_ref, q_ref, k_ref, v_ref, gate_ref, gain_ref, o_ref, acc_ref, accb_ref, *, chunk):
    hi = pl.program_id(1)
    c_len = chunk
    s_len = q_ref.shape[1]
    n_chunks = s_len // c_len
    lane = lax.broadcasted_iota(jnp.int32, (1, logit_ref.shape[1]), 1)
    lg = logit_ref[...]
    log_g = -jnp.log(1.0 + jnp.exp(-lg))
    log_g = jnp.sum(jnp.where(lane == hi, log_g, 0.0), axis=-1, keepdims=True)
    lgf, lgb = log_g[0:1, :], log_g[1:2, :]

    ii = lax.broadcasted_iota(jnp.int32, (c_len, c_len), 0).astype(F32)
    jj = lax.broadcasted_iota(jnp.int32, (c_len, c_len), 1).astype(F32)
    dist = ii - jj
    decay = jnp.where(dist >= 0, jnp.exp(lgf * jnp.maximum(dist, 0.0)),
                      jnp.exp(lgb * jnp.maximum(-dist, 0.0)))
    ci = lax.broadcasted_iota(jnp.int32, (c_len, 1), 0).astype(F32)
    qdec_f = jnp.exp(lgf * (ci + 1.0))
    kdec_f = jnp.exp(lgf * (c_len - 1.0 - ci))
    qdec_b = jnp.exp(lgb * (c_len - ci))
    kdec_b = jnp.exp(lgb * ci)
    cdec_f = jnp.exp(lgf * c_len)
    cdec_b = jnp.exp(lgb * c_len)
    dk = q_ref.shape[2]
    dv = v_ref.shape[2]

    def load(c):
        r = pl.ds(pl.multiple_of(c * c_len, c_len), c_len)
        return q_ref[0, r, :], k_ref[0, r, :], v_ref[0, r, :], r

    def fwd(c, state):
        q, k, v, r = load(c)
        qf, kf = q.astype(F32), k.astype(F32)
        sc = lax.dot_general(q, k, (((1,), (1,)), ((), ())), preferred_element_type=F32) * decay
        inner = jnp.dot(sc.astype(BF16), v, preferred_element_type=F32)
        cross = jnp.dot((qf * qdec_f).astype(BF16), state.astype(BF16), preferred_element_type=F32)
        acc_ref[r, :] = inner + cross
        kv = lax.dot_general((kf * kdec_f).astype(BF16), v, (((0,), (0,)), ((), ())),
                             preferred_element_type=F32)
        return state * cdec_f + kv

    def bwd(c, state):
        q, k, v, r = load(c)
        qf, kf = q.astype(F32), k.astype(F32)
        accb_ref[r, :] = jnp.dot((qf * qdec_b).astype(BF16), state.astype(BF16), preferred_element_type=F32)
        kv = lax.dot_general((kf * kdec_b).astype(BF16), v, (((0,), (0,)), ((), ())),
                             preferred_element_type=F32)
        return state * cdec_b + kv

    def both(i, states):
        return fwd(i, states[0]), bwd(n_chunks - 1 - i, states[1])

    zero = jnp.zeros((dk, dv), F32)
    lax.fori_loop(0, n_chunks, both, (zero, zero), unroll=4)

    ro = acc_ref[...] + accb_ref[...]
    mu = jnp.mean(ro, axis=-1, keepdims=True)
    xc = ro - mu
    y = xc * lax.rsqrt(jnp.mean(xc * xc, axis=-1, keepdims=True) + NORM_EPS) * gain_ref[...]
    o_ref[0] = (_silu(gate_ref[0].astype(F32)) * y).astype(BF16)


def _retention(proj3, gate3, decay_logit, norm_gain):
    b, s, _ = proj3.shape
    h = RET_HEADS
    base = 3 * DIFF_HEADS
    kern = functools.partial(_retention_kernel, chunk=min(RET_CHUNK, s))

    def col(off):
        return pl.BlockSpec((1, s, LANE), lambda bi, hi: (bi, 0, base + off * h + hi))

    return pl.pallas_call(
        kern,
        grid=(b, h),
        in_specs=[pl.BlockSpec((2, h), lambda bi, hi: (0, 0)), col(0), col(1), col(2),
                  pl.BlockSpec((1, s, LANE), lambda bi, hi: (bi, 0, hi)),
                  pl.BlockSpec((1, LANE), lambda bi, hi: (0, hi))],
        out_specs=pl.BlockSpec((1, s, LANE), lambda bi, hi: (bi, 0, hi)),
        out_shape=jax.ShapeDtypeStruct((b, s, h * LANE), BF16),
        scratch_shapes=[pltpu.VMEM((s, LANE), F32), pltpu.VMEM((s, LANE), F32)],
        compiler_params=_cparams(("arbitrary", "arbitrary")),
        name="retention",
    )(decay_logit, proj3, proj3, proj3, gate3, norm_gain)


def _outproj_kernel(ad_ref, ro_ref, wa_ref, wr_ref, x_ref, g1_ref, sc2_ref, sh2_ref, gpost_ref, gpre_ref,
                    wrt_ref, x1_ref, h2_ref, lt_ref):
    mix = (jnp.dot(ad_ref[...], wa_ref[...], preferred_element_type=F32)
           + jnp.dot(ro_ref[...], wr_ref[...], preferred_element_type=F32))
    x1 = x_ref[...] + g1_ref[0] * (_rms(mix) * gpost_ref[...])
    x1_ref[...] = x1
    h2 = _rms(x1) * gpre_ref[...] * (1.0 + sc2_ref[0]) + sh2_ref[0]
    h2_ref[...] = h2
    h_hi = h2.astype(BF16)
    h_lo = (h2 - h_hi.astype(F32)).astype(BF16)
    w = wrt_ref[...]
    n_e = w.shape[0]
    w_hi = w.astype(BF16)
    w_lo = (w - w_hi.astype(F32)).astype(BF16)
    nt = (((1,), (1,)), ((), ()))
    r1 = lax.dot_general(jnp.concatenate([w_hi, w_lo], axis=0), h_hi, nt, preferred_element_type=F32)
    r2 = lax.dot_general(w_hi, h_lo, nt, preferred_element_type=F32)
    lt_ref[...] = r1[0:n_e, :] + r1[n_e:, :] + r2


def _outproj(ad2, ro2, wa_bf, wr_bf, x2, gate1, scale2, shift2, gpost, gpre, w_router_t, seq):
    t, d = x2.shape
    ka, kr = ad2.shape[1], ro2.shape[1]
    e = w_router_t.shape[0]
    tm = min(seq, 512)
    tiles_per_seq = seq // tm
    mod_spec = pl.BlockSpec((1, 1, d), lambda i: (i // tiles_per_seq, 0, 0))
    vec_spec = pl.BlockSpec((1, d), lambda i: (0, 0))
    row_spec = pl.BlockSpec((tm, d), lambda i: (i, 0))
    return pl.pallas_call(
        _outproj_kernel,
        grid=(t // tm,),
        in_specs=[pl.BlockSpec((tm, ka), lambda i: (i, 0)), pl.BlockSpec((tm, kr), lambda i: (i, 0)),
                  pl.BlockSpec((ka, d), lambda i: (0, 0)), pl.BlockSpec((kr, d), lambda i: (0, 0)),
                  row_spec, mod_spec, mod_spec, mod_spec, vec_spec, vec_spec,
                  pl.BlockSpec((e, d), lambda i: (0, 0))],
        out_specs=[row_spec, row_spec, pl.BlockSpec((e, tm), lambda i: (0, i))],
        out_shape=[jax.ShapeDtypeStruct((t, d), F32), jax.ShapeDtypeStruct((t, d), F32),
                   jax.ShapeDtypeStruct((e, t), F32)],
        compiler_params=_cparams(("arbitrary",)),
        name="outproj",
    )(ad2, ro2, wa_bf, wr_bf, x2, gate1, scale2, shift2, gpost, gpre, w_router_t)


def _lane_cumsum_exclusive(x, tri):
    r, s = x.shape
    carry = jnp.zeros((r, 1), F32)
    cols = []
    for blk in range(s // LANE):
        xb = x[:, blk * LANE:(blk + 1) * LANE]
        inc = jnp.dot(xb.astype(BF16), tri, preferred_element_type=F32)
        cols.append(inc - xb + carry)
        carry = carry + inc[:, LANE - 1:LANE]
    return jnp.concatenate(cols, axis=1)


def _route_kernel(lt_ref, post_ref, info_ref, *, cap):
    logits = lt_ref[...]
    n_e, s_len = logits.shape
    mx = jnp.max(logits, axis=0, keepdims=True)
    ex = jnp.exp(logits - mx)
    aff = ex / jnp.sum(ex, axis=0, keepdims=True)

    kf = float(cap)
    tbits = jnp.zeros((n_e, 1), jnp.int32)
    for bit in range(30, -1, -1):
        cand = tbits | (1 << bit)
        cnt = jnp.sum(jnp.where(aff >= pltpu.bitcast(cand, F32), 1.0, 0.0), axis=1, keepdims=True)
        tbits = jnp.where(cnt >= kf, cand, tbits)
    thr = pltpu.bitcast(tbits, F32)

    ri = lax.broadcasted_iota(jnp.int32, (LANE, LANE), 0)
    rj = lax.broadcasted_iota(jnp.int32, (LANE, LANE), 1)
    tri = jnp.where(ri <= rj, 1.0, 0.0).astype(BF16)
    gt = jnp.where(aff > thr, 1.0, 0.0)
    tie = jnp.where(aff == thr, 1.0, 0.0)
    need = kf - jnp.sum(gt, axis=1, keepdims=True)
    tie_rank = _lane_cumsum_exclusive(tie, tri)
    sel = gt + tie * jnp.where(tie_rank < need, 1.0, 0.0)
    pos = _lane_cumsum_exclusive(sel, tri)
    pos = jnp.where(sel > 0.5, pos, -1.0)

    tok_major = jnp.concatenate([pos, jnp.zeros((LANE - n_e, s_len), F32)], axis=0).T
    post_ref[0] = tok_major.astype(jnp.int32)
    a1 = aff.astype(BF16)
    r1 = aff - a1.astype(F32)
    a2 = r1.astype(BF16)
    a3 = (r1 - a2.astype(F32)).astype(BF16)
    tok = lax.broadcasted_iota(jnp.int32, (n_e, s_len), 1)
    row = lax.broadcasted_iota(jnp.int32, (n_e, s_len), 0)
    digits = jnp.where(row == 0, tok // 64, jnp.where(row == 1, tok % 64, 0)).astype(F32).astype(BF16)
    rows = jnp.concatenate([digits, a1, a2, a3, jnp.zeros((LANE - 4 * n_e, s_len), BF16)], axis=0)
    slot = lax.broadcasted_iota(jnp.int32, (s_len, cap), 1).astype(F32)
    for e in range(n_e):
        onehot = jnp.where(tok_major[:, e:e + 1] == slot, 1.0, 0.0).astype(BF16)
        res = jnp.dot(rows, onehot, preferred_element_type=F32)
        tok_idx = res[0:1, :] * 64.0 + res[1:2, :]
        gate = (res[n_e + e:n_e + e + 1, :] + res[2 * n_e + e:2 * n_e + e + 1, :]
                + res[3 * n_e + e:3 * n_e + e + 1, :])
        info_ref[0, e] = jnp.concatenate([tok_idx, gate, jnp.zeros((6, cap), F32)], axis=0)


def _route(logits_t, b, s, cap):
    n_e = logits_t.shape[0]
    kern = functools.partial(_route_kernel, cap=cap)
    return pl.pallas_call(
        kern,
        grid=(b,),
        in_specs=[pl.BlockSpec((n_e, s), lambda bi: (0, bi))],
        out_specs=[pl.BlockSpec((1, s, LANE), lambda bi: (bi, 0, 0)),
                   pl.BlockSpec((1, n_e, 8, cap), lambda bi: (bi, 0, 0, 0))],
        out_shape=[jax.ShapeDtypeStruct((b, s, LANE), jnp.int32),
                   jax.ShapeDtypeStruct((b, n_e, 8, cap), F32)],
        compiler_params=_cparams(("arbitrary",)),
        name="route",
    )(logits_t)


def _gather_kernel(rows_ref, src_ref, o_ref, buf_ref, sem, *, rows_per_step):
    i = pl.program_id(0)

    def issue(step, slot):
        base = step * rows_per_step

        def body(k, carry):
            pltpu.make_async_copy(src_ref.at[pl.ds(rows_ref[base + k], 1)],
                                  buf_ref.at[slot, pl.ds(k, 1)], sem.at[slot]).start()
            return carry

        lax.fori_loop(0, rows_per_step, body, 0, unroll=8)

    @pl.when(i == 0)
    def _():
        issue(0, 0)

    @pl.when(i + 1 < pl.num_programs(0))
    def _():
        issue(i + 1, (i + 1) % 2)

    slot = i % 2
    pltpu.make_async_copy(src_ref.at[pl.ds(0, rows_per_step)], buf_ref.at[slot], sem.at[slot]).wait()
    o_ref[...] = buf_ref[slot].astype(o_ref.dtype)


def _gather_rows(rows, src):
    n = rows.shape[0]
    d = src.shape[1]
    rows_per_step = min(n, 512)
    kern = functools.partial(_gather_kernel, rows_per_step=rows_per_step)
    return pl.pallas_call(
        kern,
        grid_spec=pltpu.PrefetchScalarGridSpec(
            num_scalar_prefetch=1,
            grid=(n // rows_per_step,),
            in_specs=[pl.BlockSpec(memory_space=pl.ANY)],
            out_specs=pl.BlockSpec((rows_per_step, d), lambda i, rows: (i, 0)),
            scratch_shapes=[pltpu.VMEM((2, rows_per_step, d), src.dtype), pltpu.SemaphoreType.DMA((2,))]),
        out_shape=jax.ShapeDtypeStruct((n, d), BF16),
        compiler_params=_cparams(("arbitrary",)),
        name="gather",
    )(rows, src)


def _ffn_kernel(x_ref, gate_ref, wg_ref, wu_ref, wd_ref, y_ref, acc_ref):
    f = pl.program_id(2)

    @pl.when(f == 0)
    def _():
        acc_ref[...] = jnp.zeros_like(acc_ref)

    xb = x_ref[...]
    a = jnp.dot(xb, wg_ref[0].astype(BF16), preferred_element_type=F32)
    u = jnp.dot(xb, wu_ref[0].astype(BF16), preferred_element_type=F32)
    hmid = (_silu(a) * u).astype(BF16)
    acc_ref[...] += jnp.dot(hmid, wd_ref[0].astype(BF16), preferred_element_type=F32)

    @pl.when(f == pl.num_programs(2) - 1)
    def _():
        y_ref[...] = (acc_ref[...] * gate_ref[...]).astype(BF16)


def _ffn(xg, gate_col, w_gate, w_up, w_down, rows_per_expert):
    n, d = xg.shape
    n_e, _, ff = w_gate.shape
    tm = min(rows_per_expert, 1024)
    tf = min(ff, 512)
    tiles_per_expert = rows_per_expert // tm
    row_map = lambda e, r, f: (e * tiles_per_expert + r, 0)
    return pl.pallas_call(
        _ffn_kernel,
        grid=(n_e, tiles_per_expert, ff // tf),
        in_specs=[pl.BlockSpec((tm, d), row_map), pl.BlockSpec((tm, 1), row_map),
                  pl.BlockSpec((1, d, tf), lambda e, r, f: (e, 0, f)),
                  pl.BlockSpec((1, d, tf), lambda e, r, f: (e, 0, f)),
                  pl.BlockSpec((1, tf, d), lambda e, r, f: (e, f, 0))],
        out_specs=pl.BlockSpec((tm, d), row_map),
        out_shape=jax.ShapeDtypeStruct((n, d), BF16),
        scratch_shapes=[pltpu.VMEM((tm, d), F32)],
        compiler_params=_cparams(("arbitrary", "arbitrary", "arbitrary")),
        name="ffn",
    )(xg, gate_col, w_gate, w_up, w_down)


def _combine_kernel(post_ref, *refs, experts_per_step):
    y_refs = refs[:experts_per_step]
    x1_ref, g2_ref, gpost_ref, o_ref, acc_ref = refs[experts_per_step:]
    eg = pl.program_id(2)

    @pl.when(eg == 0)
    def _():
        acc_ref[...] = jnp.zeros_like(acc_ref)

    posmat = post_ref[0]
    lane = lax.broadcasted_iota(jnp.int32, posmat.shape, 1)
    cap = y_refs[0].shape[0]
    kw = min(cap, 256)
    total = None
    spill = []
    for k, y_ref in enumerate(y_refs):
        e = eg * experts_per_step + k
        pos = jnp.sum(jnp.where(lane == e, posmat, 0), axis=1, keepdims=True)
        hi = jnp.max(pos)
        lo = jnp.min(jnp.where(pos >= 0, pos, cap))
        start = pl.multiple_of(jnp.minimum((lo // 16) * 16, cap - kw), 16)
        slot = lax.broadcasted_iota(jnp.int32, (pos.shape[0], kw), 1) + start
        onehot = jnp.where(pos == slot, 1.0, 0.0).astype(BF16)
        part = jnp.dot(onehot, y_ref[pl.ds(start, kw), :], preferred_element_type=F32)
        total = part if total is None else total + part
        spill.append((pos, hi, start, y_ref))
    acc_ref[...] += total

    for pos, hi, start, y_ref in spill:
        @pl.when(hi >= start + kw)
        def _(pos=pos, start=start, y_ref=y_ref):
            slot = lax.broadcasted_iota(jnp.int32, (pos.shape[0], cap), 1)
            onehot = jnp.where((pos == slot) & (slot >= start + kw), 1.0, 0.0).astype(BF16)
            acc_ref[...] += jnp.dot(onehot, y_ref[...], preferred_element_type=F32)

    @pl.when(eg == pl.num_programs(2) - 1)
    def _():
        o_ref[...] = x1_ref[...] + g2_ref[0] * (_rms(acc_ref[...]) * gpost_ref[...])


def _combine(pos_t, y, x1, gate2, gpost, b, s, cap):
    t, d = x1.shape
    n_e = y.shape[0] // (b * cap)
    tt = min(s, 512)
    tiles_per_seq = s // tt
    eps = 4
    assert n_e % eps == 0
    y_specs = [pl.BlockSpec((cap, d), lambda bi, ji, eg, k=k: ((eg * eps + k) * b + bi, 0)) for k in range(eps)]
    return pl.pallas_call(
        functools.partial(_combine_kernel, experts_per_step=eps),
        grid=(b, tiles_per_seq, n_e // eps),
        in_specs=[pl.BlockSpec((1, tt, LANE), lambda bi, ji, eg: (bi, ji, 0))] + y_specs + [
            pl.BlockSpec((tt, d), lambda bi, ji, eg: (bi * tiles_per_seq + ji, 0)),
            pl.BlockSpec((1, 1, d), lambda bi, ji, eg: (bi, 0, 0)),
            pl.BlockSpec((1, d), lambda bi, ji, eg: (0, 0))],
        out_specs=pl.BlockSpec((tt, d), lambda bi, ji, eg: (bi * tiles_per_seq + ji, 0)),
        out_shape=jax.ShapeDtypeStruct((t, d), F32),
        scratch_shapes=[pltpu.VMEM((tt, d), F32)],
        compiler_params=_cparams(("arbitrary", "arbitrary", "arbitrary")),
        name="combine",
    )(pos_t, *([y] * eps), x1, gate2, gpost)


def kernel(x, c, positions, w_ada, b_ada, g_pre_mix, g_post_mix, w_in, diff_lambda_q1, diff_lambda_k1,
           diff_lambda_q2, diff_lambda_k2, diff_subln_gain, ret_decay_logit, ret_norm_gain, w_out, g_pre_ffn,
           g_post_ffn, w_router, w_gate, w_up, w_down):
    b, s, d = x.shape
    t = b * s
    depth = w_ada.shape[0]
    n_e = w_router.shape[2]
    cap = EC_CAPACITY_FACTOR * s // n_e
    diff_w = DIFF_HEADS * LANE

    pos_col = positions.reshape(t, 1)
    tabs = _rope_tables(pos_col)
    c_pad = jnp.pad(c, ((0, (-b) % 8), (0, 0)))
    x2 = x.reshape(t, d)
    for l in range(depth):
        lam_init = 0.8 - 0.6 * math.exp(-0.3 * l)
        mod = _ada(c_pad, w_ada[l], b_ada[l].reshape(1, -1))[:b]
        shift1, scale1, gate1, shift2, scale2, gate2 = [m.reshape(b, 1, d) for m in jnp.split(mod, 6, axis=-1)]

        proj, rgate = _inproj(x2, scale1, shift1, g_pre_mix[l].reshape(1, d), w_in[l].astype(BF16), tabs, s)
        proj3 = proj.reshape(b, s, -1)
        lams = [v[l].reshape(1, -1) for v in (diff_lambda_q1, diff_lambda_k1, diff_lambda_q2, diff_lambda_k2)]
        ad = _diffattn(proj3, lams, diff_subln_gain[l].reshape(1, -1), lam_init)
        ro = _retention(proj3, rgate.reshape(b, s, -1), ret_decay_logit[l], ret_norm_gain[l].reshape(1, -1))

        w_out_bf = w_out[l].astype(BF16)
        x1, h2, logits_t = _outproj(
            ad.reshape(t, -1), ro.reshape(t, -1), w_out_bf[:diff_w], w_out_bf[diff_w:], x2, gate1, scale2,
            shift2, g_post_mix[l].reshape(1, d), g_pre_ffn[l].reshape(1, d), w_router[l].T, s)

        pos_t, info = _route(logits_t, b, s, cap)
        rows = info[:, :, 0, :].astype(jnp.int32) + (jnp.arange(b, dtype=jnp.int32) * s)[:, None, None]
        rows = jnp.transpose(rows, (1, 0, 2)).reshape(-1)
        gate_col = jnp.transpose(info[:, :, 1, :], (1, 0, 2)).reshape(-1, 1)
        xg = _gather_rows(rows, h2)
        y = _ffn(xg, gate_col, w_gate[l], w_up[l], w_down[l], b * cap)
        x2 = _combine(pos_t, y, x1, gate2, g_post_ffn[l].reshape(1, d), b, s, cap)
    return x2.reshape(b, s, d)
```

```python
import functools
import math

import jax
import jax.numpy as jnp
from jax import lax
from jax.experimental import pallas as pl
from jax.experimental.pallas import tpu as pltpu

F32 = jnp.float32
BF16 = jnp.bfloat16
HIGHEST = lax.Precision.HIGHEST

DIFF_HEADS = 8
DIFF_HEAD_DIM = 64
RET_HEADS = 8
RET_QK_DIM = 128
LANE = 128
N_EXPERTS = 16
EC_CAPACITY_FACTOR = 2
ROPE_THETA = 10000.0
NORM_EPS = 1e-6
RET_CHUNK = 256
VMEM_LIMIT = 56 * 1024 * 1024


def _cparams(sem):
    return pltpu.CompilerParams(dimension_semantics=sem, vmem_limit_bytes=VMEM_LIMIT)


def _silu(x):
    return x / (1.0 + jnp.exp(-x))


def _rms(x, eps=NORM_EPS):
    return x * lax.rsqrt(jnp.mean(x * x, axis=-1, keepdims=True) + eps)


def _ada_kernel(c_ref, w_ref, b_ref, o_ref):
    s = _silu(c_ref[...])
    o_ref[...] = jnp.dot(s, w_ref[...], precision=HIGHEST, preferred_element_type=F32) + b_ref[...]


def _ada(c_pad, w_ada, b_ada):
    m, d = c_pad.shape
    n = w_ada.shape[1]
    tn = min(d, 1024)
    assert n % tn == 0
    return pl.pallas_call(
        _ada_kernel,
        grid=(n // tn,),
        in_specs=[pl.BlockSpec((m, d), lambda j: (0, 0)),
                  pl.BlockSpec((d, tn), lambda j: (0, j)),
                  pl.BlockSpec((1, tn), lambda j: (0, j))],
        out_specs=pl.BlockSpec((m, tn), lambda j: (0, j)),
        out_shape=jax.ShapeDtypeStruct((m, n), F32),
        compiler_params=_cparams(("arbitrary",)),
        name="ada",
    )(c_pad, w_ada, b_ada)


def _rope_tab_kernel(pos_ref, cd_ref, sda_ref, sdb_ref, cr_ref, sr_ref):
    pos = pos_ref[...].astype(F32)
    lane = lax.broadcasted_iota(jnp.int32, (pos.shape[0], LANE), 1)
    is_d = lane < 32
    expo = jnp.where(is_d, lane.astype(F32) * (2.0 / DIFF_HEAD_DIM),
                     (lane - 32).astype(F32) * (2.0 / RET_QK_DIM))
    inv = jnp.exp(expo * (-math.log(ROPE_THETA)))
    ang = pos * inv
    c = jnp.cos(ang)
    s = jnp.sin(ang)
    c32, c64, c96 = pltpu.roll(c, 32, 1), pltpu.roll(c, 64, 1), pltpu.roll(c, 96, 1)
    s32, s64, s96 = pltpu.roll(s, 32, 1), pltpu.roll(s, 64, 1), pltpu.roll(s, 96, 1)
    q = lane // 32
    cos_d = jnp.where(q == 0, c, jnp.where(q == 1, c32, jnp.where(q == 2, c64, c96)))
    sin_d = jnp.where(q == 0, s, jnp.where(q == 1, s32, jnp.where(q == 2, s64, s96)))
    first_half = (q == 0) | (q == 2)
    cd_ref[...] = cos_d
    sda_ref[...] = jnp.where(first_half, -sin_d, 0.0)
    sdb_ref[...] = jnp.where(first_half, 0.0, sin_d)
    lo = lane < 64
    cr_ref[...] = jnp.where(lo, c96, c32)
    sr_ref[...] = jnp.where(lo, -s96, s32)


def _rope_tables(pos_col):
    t = pos_col.shape[0]
    tm = min(t, 2048)
    spec = pl.BlockSpec((tm, LANE), lambda i: (i, 0))
    return pl.pallas_call(
        _rope_tab_kernel,
        grid=(t // tm,),
        in_specs=[pl.BlockSpec((tm, 1), lambda i: (i, 0))],
        out_specs=[spec] * 5,
        out_shape=[jax.ShapeDtypeStruct((t, LANE), F32)] * 5,
        compiler_params=_cparams(("arbitrary",)),
        name="rope_tab",
    )(pos_col)


def _inproj_kernel(x_ref, sc_ref, sh_ref, g_ref, w_ref, cd_ref, sda_ref, sdb_ref, cr_ref, sr_ref,
                   o_ref, og_ref, h_ref, acc_ref, *, n_col, q_scale, rk_scale):
    j = pl.program_id(1)
    nslab = o_ref.shape[1] // LANE

    def epilogue(kind, slot):
        if kind in (0, 1):
            scale = q_scale if kind == 0 else 1.0
            cd, sa, sb = cd_ref[...], sda_ref[...], sdb_ref[...]
            for s in range(nslab):
                xs = acc_ref[slot, :, s * LANE:(s + 1) * LANE]
                y = xs * cd + pltpu.roll(xs, 96, 1) * sa + pltpu.roll(xs, 32, 1) * sb
                o_ref[:, s * LANE:(s + 1) * LANE] = (y * scale).astype(BF16)
        elif kind in (3, 4):
            scale = rk_scale if kind == 4 else 1.0
            cr, sr = cr_ref[...], sr_ref[...]
            for s in range(nslab):
                xs = acc_ref[slot, :, s * LANE:(s + 1) * LANE]
                y = xs * cr + pltpu.roll(xs, 64, 1) * sr
                o_ref[:, s * LANE:(s + 1) * LANE] = (y * scale).astype(BF16)
        else:
            o_ref[...] = acc_ref[slot].astype(BF16)

    for jj in range(n_col):
        @pl.when(j == jj)
        def _(jj=jj):
            if jj == 0:
                h = _rms(x_ref[...]) * g_ref[...] * (1.0 + sc_ref[0]) + sh_ref[0]
                h_ref[...] = h.astype(BF16)
            acc_ref[jj % 2] = jnp.dot(h_ref[...], w_ref[...], preferred_element_type=F32)
            if jj > 0:
                epilogue(jj - 1, (jj - 1) % 2)
            if jj == n_col - 1:
                og_ref[...] = acc_ref[jj % 2].astype(BF16)


def _inproj(x2, scale1, shift1, g, w_bf, tabs, seq):
    t, d = x2.shape
    nc = w_bf.shape[1]
    n_col = 7
    tn = nc // n_col
    tm = min(seq, 1024)
    tiles_per_seq = seq // tm
    kern = functools.partial(
        _inproj_kernel, n_col=n_col,
        q_scale=(DIFF_HEAD_DIM ** -0.5) * math.log2(math.e), rk_scale=RET_QK_DIM ** -0.5)
    mod_spec = pl.BlockSpec((1, 1, d), lambda i, j: (i // tiles_per_seq, 0, 0))
    tab_spec = pl.BlockSpec((tm, LANE), lambda i, j: (i, 0))
    return pl.pallas_call(
        kern,
        grid=(t // tm, n_col),
        in_specs=[pl.BlockSpec((tm, d), lambda i, j: (i, 0)), mod_spec, mod_spec,
                  pl.BlockSpec((1, d), lambda i, j: (0, 0)),
                  pl.BlockSpec((d, tn), lambda i, j: (0, j))] + [tab_spec] * 5,
        out_specs=[pl.BlockSpec((tm, tn), lambda i, j: (i, jnp.maximum(j - 1, 0))),
                   pl.BlockSpec((tm, tn), lambda i, j: (i, 0))],
        out_shape=[jax.ShapeDtypeStruct((t, nc - tn), BF16), jax.ShapeDtypeStruct((t, tn), BF16)],
        scratch_shapes=[pltpu.VMEM((tm, d), BF16), pltpu.VMEM((2, tm, tn), F32)],
        compiler_params=_cparams(("arbitrary", "arbitrary")),
        name="inproj",
    )(x2, scale1, shift1, g, w_bf, *tabs)


def _diffattn_kernel(lq1_ref, lk1_ref, lq2_ref, lk2_ref, q_ref, k_ref, v_ref, g_ref, o_ref,
                     vt_ref, s_ref, p_ref, mx_ref, l_ref, *, lam_init, tk):
    i = pl.program_id(2)
    nq = pl.num_programs(2) - 2
    dv = v_ref.shape[2]
    tq = q_ref.shape[1]
    s_len = k_ref.shape[1]
    nk = s_len // tk

    @pl.when(i == 0)
    def _():
        vt_ref[...] = v_ref[0].astype(F32).T.astype(BF16)

    def stages(do_score, do_exp, do_value):
        cur = i % 2
        prev = 1 - cur
        if do_score:
            q = q_ref[0]
            lane = lax.broadcasted_iota(jnp.int32, q.shape, 1)
            qm = [jnp.where((lane >= m * DIFF_HEAD_DIM) & (lane < (m + 1) * DIFF_HEAD_DIM), q, jnp.zeros_like(q))
                  for m in range(2)]
            cmax = [jnp.full((8, tq), -jnp.inf, F32) for _ in range(2)]
        if do_exp:
            mx = [jnp.max(mx_ref[prev, m], axis=0, keepdims=True) for m in range(2)]
            lsum = [jnp.zeros((8, tq), F32) for _ in range(2)]
        if do_value:
            lam = (jnp.exp(jnp.sum(lq1_ref[...] * lk1_ref[...], axis=-1, keepdims=True))
                   - jnp.exp(jnp.sum(lq2_ref[...] * lk2_ref[...], axis=-1, keepdims=True)) + lam_init)
            den = [jnp.sum(l_ref[cur, m], axis=0, keepdims=True) for m in range(2)]
            ratio = (lam * den[0] / den[1]).astype(BF16)
            acc = jnp.zeros((dv, tq), F32)
        for c in range(nk):
            rows = slice(c * tk, (c + 1) * tk)
            if do_value:
                w = p_ref[0, rows, :] - p_ref[1, rows, :] * ratio
                acc = acc + jnp.dot(vt_ref[:, rows], w, preferred_element_type=F32)
            if do_exp:
                for m in range(2):
                    p32 = jnp.exp2(s_ref[m, rows, :] - mx[m])
                    lsum[m] = lsum[m] + jnp.sum(p32.reshape(tk // 8, 8, tq), axis=0)
                    p_ref[m, rows, :] = p32.astype(BF16)
            if do_score:
                for m in range(2):
                    sc = lax.dot_general(k_ref[0, rows, :], qm[m], (((1,), (1,)), ((), ())),
                                         preferred_element_type=F32)
                    s_ref[m, rows, :] = sc
                    cmax[m] = jnp.maximum(cmax[m], jnp.max(sc.reshape(tk // 8, 8, tq), axis=0))
        if do_score:
            for m in range(2):
                mx_ref[cur, m] = cmax[m]
        if do_exp:
            for m in range(2):
                l_ref[prev, m] = lsum[m]
        if do_value:
            ad = acc / den[0]
            y = ad * lax.rsqrt(jnp.mean(ad * ad, axis=0, keepdims=True) + NORM_EPS)
            o_ref[0] = (y.T * (g_ref[...] * (1.0 - lam_init))).astype(BF16)

    pl.when(i == 0)(lambda: stages(True, False, False))
    pl.when((i == 1) & (i < nq))(lambda: stages(True, True, False))
    pl.when((i >= 2) & (i < nq))(lambda: stages(True, True, True))
    pl.when((i == nq) & (i == 1))(lambda: stages(False, True, False))
    pl.when((i == nq) & (i >= 2))(lambda: stages(False, True, True))
    pl.when(i == nq + 1)(lambda: stages(False, False, True))


def _diffattn(proj3, lams, subln_gain, lam_init):
    b, s, _ = proj3.shape
    h = DIFF_HEADS
    tq = min(s, 512)
    tk = min(s, 512)
    nq = s // tq
    kern = functools.partial(_diffattn_kernel, lam_init=lam_init, tk=tk)
    lam_spec = pl.BlockSpec((1, DIFF_HEAD_DIM), lambda bi, hi, i: (0, 0))
    return pl.pallas_call(
        kern,
        grid=(b, h, nq + 2),
        in_specs=[lam_spec] * 4 + [
            pl.BlockSpec((1, tq, LANE), lambda bi, hi, i: (bi, jnp.minimum(i, nq - 1), hi)),
            pl.BlockSpec((1, s, LANE), lambda bi, hi, i: (bi, 0, h + hi)),
            pl.BlockSpec((1, s, LANE), lambda bi, hi, i: (bi, 0, 2 * h + hi)),
            pl.BlockSpec((1, LANE), lambda bi, hi, i: (0, 0))],
        out_specs=pl.BlockSpec((1, tq, LANE), lambda bi, hi, i: (bi, jnp.maximum(i - 2, 0), hi)),
        out_shape=jax.ShapeDtypeStruct((b, s, h * LANE), BF16),
        scratch_shapes=[pltpu.VMEM((LANE, s), BF16), pltpu.VMEM((2, s, tq), F32), pltpu.VMEM((2, s, tq), BF16),
                        pltpu.VMEM((2, 2, 8, tq), F32), pltpu.VMEM((2, 2, 8, tq), F32)],
        compiler_params=_cparams(("arbitrary", "arbitrary", "arbitrary")),
        name="diffattn",
    )(*lams, proj3, proj3, proj3, subln_gain)


def _retention_kernel(logit_ref, q_ref, k_ref, v_ref, gate_ref, gain_ref, o_ref, acc_ref, accb_ref, *, chunk):
    hi = pl.program_id(1)
    c_len = chunk
    s_len = q_ref.shape[1]
    n_chunks = s_len // c_len
    lane = lax.broadcasted_iota(jnp.int32, (1, logit_ref.shape[1]), 1)
    lg = logit_ref[...]
    log_g = -jnp.log(1.0 + jnp.exp(-lg))
    log_g = jnp.sum(jnp.where(lane == hi, log_g, 0.0), axis=-1, keepdims=True)
    lgf, lgb = log_g[0:1, :], log_g[1:2, :]

    ii = lax.broadcasted_iota(jnp.int32, (c_len, c_len), 0).astype(F32)
    jj = lax.broadcasted_iota(jnp.int32, (c_len, c_len), 1).astype(F32)
    dist = ii - jj
    decay = jnp.where(dist >= 0, jnp.exp(lgf * jnp.maximum(dist, 0.0)),
                      jnp.exp(lgb * jnp.maximum(-dist, 0.0)))
    ci = lax.broadcasted_iota(jnp.int32, (c_len, 1), 0).astype(F32)
    qdec_f = jnp.exp(lgf * (ci + 1.0))
    kdec_f = jnp.exp(lgf * (c_len - 1.0 - ci))
    qdec_b = jnp.exp(lgb * (c_len - ci))
    kdec_b = jnp.exp(lgb * ci)
    cdec_f = jnp.exp(lgf * c_len)
    cdec_b = jnp.exp(lgb * c_len)
    dk = q_ref.shape[2]
    dv = v_ref.shape[2]

    def load(c):
        r = pl.ds(pl.multiple_of(c * c_len, c_len), c_len)
        return q_ref[0, r, :], k_ref[0, r, :], v_ref[0, r, :], r

    def fwd(c, state):
        q, k, v, r = load(c)
        qf, kf = q.astype(F32), k.astype(F32)
        sc = lax.dot_general(q, k, (((1,), (1,)), ((), ())), preferred_element_type=F32) * decay
        inner = jnp.dot(sc.astype(BF16), v, preferred_element_type=F32)
        cross = jnp.dot((qf * qdec_f).astype(BF16), state.astype(BF16), preferred_element_type=F32)
        acc_ref[r, :] = inner + cross
        kv = lax.dot_general((kf * kdec_f).astype(BF16), v, (((0,), (0,)), ((), ())),
                             preferred_element_type=F32)
        return state * cdec_f + kv

    def bwd(c, state):
        q, k, v, r = load(c)
        qf, kf = q.astype(F32), k.astype(F32)
        accb_ref[r, :] = jnp.dot((qf * qdec_b).astype(BF16), state.astype(BF16), preferred_element_type=F32)
        kv = lax.dot_general((kf * kdec_b).astype(BF16), v, (((0,), (0,)), ((), ())),
                             preferred_element_type=F32)
        return state * cdec_b + kv

    def both(i, states):
        return fwd(i, states[0]), bwd(n_chunks - 1 - i, states[1])

    zero = jnp.zeros((dk, dv), F32)
    lax.fori_loop(0, n_chunks, both, (zero, zero), unroll=4)

    ro = acc_ref[...] + accb_ref[...]
    mu = jnp.mean(ro, axis=-1, keepdims=True)
    xc = ro - mu
    y = xc * lax.rsqrt(jnp.mean(xc * xc, axis=-1, keepdims=True) + NORM_EPS) * gain_ref[...]
    o_ref[0] = (_silu(gate_ref[0].astype(F32)) * y).astype(BF16)


def _retention(proj3, gate3, decay_logit, norm_gain):
    b, s, _ = proj3.shape
    h = RET_HEADS
    base = 3 * DIFF_HEADS
    kern = functools.partial(_retention_kernel, chunk=min(RET_CHUNK, s))

    def col(off):
        return pl.BlockSpec((1, s, LANE), lambda bi, hi: (bi, 0, base + off * h + hi))

    return pl.pallas_call(
        kern,
        grid=(b, h),
        in_specs=[pl.BlockSpec((2, h), lambda bi, hi: (0, 0)), col(0), col(1), col(2),
                  pl.BlockSpec((1, s, LANE), lambda bi, hi: (bi, 0, hi)),
                  pl.BlockSpec((1, LANE), lambda bi, hi: (0, hi))],
        out_specs=pl.BlockSpec((1, s, LANE), lambda bi, hi: (bi, 0, hi)),
        out_shape=jax.ShapeDtypeStruct((b, s, h * LANE), BF16),
        scratch_shapes=[pltpu.VMEM((s, LANE), F32), pltpu.VMEM((s, LANE), F32)],
        compiler_params=_cparams(("arbitrary", "arbitrary")),
        name="retention",
    )(decay_logit, proj3, proj3, proj3, gate3, norm_gain)


def _outproj_kernel(ad_ref, ro_ref, wa_ref, wr_ref, x_ref, g1_ref, sc2_ref, sh2_ref, gpost_ref, gpre_ref,
                    wrt_ref, x1_ref, h2_ref, lt_ref):
    mix = (jnp.dot(ad_ref[...], wa_ref[...], preferred_element_type=F32)
           + jnp.dot(ro_ref[...], wr_ref[...], preferred_element_type=F32))
    x1 = x_ref[...] + g1_ref[0] * (_rms(mix) * gpost_ref[...])
    x1_ref[...] = x1
    h2 = _rms(x1) * gpre_ref[...] * (1.0 + sc2_ref[0]) + sh2_ref[0]
    h2_ref[...] = h2
    h_hi = h2.astype(BF16)
    h_lo = (h2 - h_hi.astype(F32)).astype(BF16)
    w = wrt_ref[...]
    n_e = w.shape[0]
    w_hi = w.astype(BF16)
    w_lo = (w - w_hi.astype(F32)).astype(BF16)
    nt = (((1,), (1,)), ((), ()))
    r1 = lax.dot_general(jnp.concatenate([w_hi, w_lo], axis=0), h_hi, nt, preferred_element_type=F32)
    r2 = lax.dot_general(w_hi, h_lo, nt, preferred_element_type=F32)
    lt_ref[...] = r1[0:n_e, :] + r1[n_e:, :] + r2


def _outproj(ad2, ro2, wa_bf, wr_bf, x2, gate1, scale2, shift2, gpost, gpre, w_router_t, seq):
    t, d = x2.shape
    ka, kr = ad2.shape[1], ro2.shape[1]
    e = w_router_t.shape[0]
    tm = min(seq, 512)
    tiles_per_seq = seq // tm
    mod_spec = pl.BlockSpec((1, 1, d), lambda i: (i // tiles_per_seq, 0, 0))
    vec_spec = pl.BlockSpec((1, d), lambda i: (0, 0))
    row_spec = pl.BlockSpec((tm, d), lambda i: (i, 0))
    return pl.pallas_call(
        _outproj_kernel,
        grid=(t // tm,),
        in_specs=[pl.BlockSpec((tm, ka), lambda i: (i, 0)), pl.BlockSpec((tm, kr), lambda i: (i, 0)),
                  pl.BlockSpec((ka, d), lambda i: (0, 0)), pl.BlockSpec((kr, d), lambda i: (0, 0)),
                  row_spec, mod_spec, mod_spec, mod_spec, vec_spec, vec_spec,
                  pl.BlockSpec((e, d), lambda i: (0, 0))],
        out_specs=[row_spec, row_spec, pl.BlockSpec((e, tm), lambda i: (0, i))],
        out_shape=[jax.ShapeDtypeStruct((t, d), F32), jax.ShapeDtypeStruct((t, d), F32),
                   jax.ShapeDtypeStruct((e, t), F32)],
        compiler_params=_cparams(("arbitrary",)),
        name="outproj",
    )(ad2, ro2, wa_bf, wr_bf, x2, gate1, scale2, shift2, gpost, gpre, w_router_t)


def _lane_cumsum_exclusive(x, tri):
    r, s = x.shape
    carry = jnp.zeros((r, 1), F32)
    cols = []
    for blk in range(s // LANE):
        xb = x[:, blk * LANE:(blk + 1) * LANE]
        inc = jnp.dot(xb.astype(BF16), tri, preferred_element_type=F32)
        cols.append(inc - xb + carry)
        carry = carry + inc[:, LANE - 1:LANE]
    return jnp.concatenate(cols, axis=1)


def _route_kernel(lt_ref, post_ref, info_ref, *, cap):
    logits = lt_ref[...]
    n_e, s_len = logits.shape
    mx = jnp.max(logits, axis=0, keepdims=True)
    ex = jnp.exp(logits - mx)
    aff = ex / jnp.sum(ex, axis=0, keepdims=True)

    kf = float(cap)
    tbits = jnp.zeros((n_e, 1), jnp.int32)
    for bit in range(30, -1, -1):
        cand = tbits | (1 << bit)
        cnt = jnp.sum(jnp.where(aff >= pltpu.bitcast(cand, F32), 1.0, 0.0), axis=1, keepdims=True)
        tbits = jnp.where(cnt >= kf, cand, tbits)
    thr = pltpu.bitcast(tbits, F32)

    ri = lax.broadcasted_iota(jnp.int32, (LANE, LANE), 0)
    rj = lax.broadcasted_iota(jnp.int32, (LANE, LANE), 1)
    tri = jnp.where(ri <= rj, 1.0, 0.0).astype(BF16)
    gt = jnp.where(aff > thr, 1.0, 0.0)
    tie = jnp.where(aff == thr, 1.0, 0.0)
    need = kf - jnp.sum(gt, axis=1, keepdims=True)
    tie_rank = _lane_cumsum_exclusive(tie, tri)
    sel = gt + tie * jnp.where(tie_rank < need, 1.0, 0.0)
    pos = _lane_cumsum_exclusive(sel, tri)
    pos = jnp.where(sel > 0.5, pos, -1.0)

    tok_major = jnp.concatenate([pos, jnp.zeros((LANE - n_e, s_len), F32)], axis=0).T
    post_ref[0] = tok_major.astype(jnp.int32)
    a1 = aff.astype(BF16)
    r1 = aff - a1.astype(F32)
    a2 = r1.astype(BF16)
    a3 = (r1 - a2.astype(F32)).astype(BF16)
    tok = lax.broadcasted_iota(jnp.int32, (n_e, s_len), 1)
    row = lax.broadcasted_iota(jnp.int32, (n_e, s_len), 0)
    digits = jnp.where(row == 0, tok // 64, jnp.where(row == 1, tok % 64, 0)).astype(F32).astype(BF16)
    rows = jnp.concatenate([digits, a1, a2, a3, jnp.zeros((LANE - 4 * n_e, s_len), BF16)], axis=0)
    slot = lax.broadcasted_iota(jnp.int32, (s_len, cap), 1).astype(F32)
    for e in range(n_e):
        onehot = jnp.where(tok_major[:, e:e + 1] == slot, 1.0, 0.0).astype(BF16)
        res = jnp.dot(rows, onehot, preferred_element_type=F32)
        tok_idx = res[0:1, :] * 64.0 + res[1:2, :]
        gate = (res[n_e + e:n_e + e + 1, :] + res[2 * n_e + e:2 * n_e + e + 1, :]
                + res[3 * n_e + e:3 * n_e + e + 1, :])
        info_ref[0, e] = jnp.concatenate([tok_idx, gate, jnp.zeros((6, cap), F32)], axis=0)


def _route(logits_t, b, s, cap):
    n_e = logits_t.shape[0]
    kern = functools.partial(_route_kernel, cap=cap)
    return pl.pallas_call(
        kern,
        grid=(b,),
        in_specs=[pl.BlockSpec((n_e, s), lambda bi: (0, bi))],
        out_specs=[pl.BlockSpec((1, s, LANE), lambda bi: (bi, 0, 0)),
                   pl.BlockSpec((1, n_e, 8, cap), lambda bi: (bi, 0, 0, 0))],
        out_shape=[jax.ShapeDtypeStruct((b, s, LANE), jnp.int32),
                   jax.ShapeDtypeStruct((b, n_e, 8, cap), F32)],
        compiler_params=_cparams(("arbitrary",)),
        name="route",
    )(logits_t)


def _gather_kernel(rows_ref, src_ref, o_ref, buf_ref, sem, *, rows_per_step):
    i = pl.program_id(0)

    def issue(step, slot):
        base = step * rows_per_step

        def body(k, carry):
            pltpu.make_async_copy(src_ref.at[pl.ds(rows_ref[base + k], 1)],
                                  buf_ref.at[slot, pl.ds(k, 1)], sem.at[slot]).start()
            return carry

        lax.fori_loop(0, rows_per_step, body, 0, unroll=8)

    @pl.when(i == 0)
    def _():
        issue(0, 0)

    @pl.when(i + 1 < pl.num_programs(0))
    def _():
        issue(i + 1, (i + 1) % 2)

    slot = i % 2
    pltpu.make_async_copy(src_ref.at[pl.ds(0, rows_per_step)], buf_ref.at[slot], sem.at[slot]).wait()
    o_ref[...] = buf_ref[slot].astype(o_ref.dtype)


def _gather_rows(rows, src):
    n = rows.shape[0]
    d = src.shape[1]
    rows_per_step = min(n, 512)
    kern = functools.partial(_gather_kernel, rows_per_step=rows_per_step)
    return pl.pallas_call(
        kern,
        grid_spec=pltpu.PrefetchScalarGridSpec(
            num_scalar_prefetch=1,
            grid=(n // rows_per_step,),
            in_specs=[pl.BlockSpec(memory_space=pl.ANY)],
            out_specs=pl.BlockSpec((rows_per_step, d), lambda i, rows: (i, 0)),
            scratch_shapes=[pltpu.VMEM((2, rows_per_step, d), src.dtype), pltpu.SemaphoreType.DMA((2,))]),
        out_shape=jax.ShapeDtypeStruct((n, d), BF16),
        compiler_params=_cparams(("arbitrary",)),
        name="gather",
    )(rows, src)


def _ffn_kernel(x_ref, gate_ref, wg_ref, wu_ref, wd_ref, y_ref, acc_ref):
    f = pl.program_id(2)

    @pl.when(f == 0)
    def _():
        acc_ref[...] = jnp.zeros_like(acc_ref)

    xb = x_ref[...]
    a = jnp.dot(xb, wg_ref[0].astype(BF16), preferred_element_type=F32)
    u = jnp.dot(xb, wu_ref[0].astype(BF16), preferred_element_type=F32)
    hmid = (_silu(a) * u).astype(BF16)
    acc_ref[...] += jnp.dot(hmid, wd_ref[0].astype(BF16), preferred_element_type=F32)

    @pl.when(f == pl.num_programs(2) - 1)
    def _():
        y_ref[...] = (acc_ref[...] * gate_ref[...]).astype(BF16)


def _ffn(xg, gate_col, w_gate, w_up, w_down, rows_per_expert):
    n, d = xg.shape
    n_e, _, ff = w_gate.shape
    tm = min(rows_per_expert, 1024)
    tf = min(ff, 512)
    tiles_per_expert = rows_per_expert // tm
    row_map = lambda e, r, f: (e * tiles_per_expert + r, 0)
    return pl.pallas_call(
        _ffn_kernel,
        grid=(n_e, tiles_per_expert, ff // tf),
        in_specs=[pl.BlockSpec((tm, d), row_map), pl.BlockSpec((tm, 1), row_map),
                  pl.BlockSpec((1, d, tf), lambda e, r, f: (e, 0, f)),
                  pl.BlockSpec((1, d, tf), lambda e, r, f: (e, 0, f)),
                  pl.BlockSpec((1, tf, d), lambda e, r, f: (e, f, 0))],
        out_specs=pl.BlockSpec((tm, d), row_map),
        out_shape=jax.ShapeDtypeStruct((n, d), BF16),
        scratch_shapes=[pltpu.VMEM((tm, d), F32)],
        compiler_params=_cparams(("arbitrary", "arbitrary", "arbitrary")),
        name="ffn",
    )(xg, gate_col, w_gate, w_up, w_down)


def _combine_kernel(post_ref, *refs, experts_per_step):
    y_refs = refs[:experts_per_step]
    x1_ref, g2_ref, gpost_ref, o_ref, acc_ref = refs[experts_per_step:]
    eg = pl.program_id(2)

    @pl.when(eg == 0)
    def _():
        acc_ref[...] = jnp.zeros_like(acc_ref)

    posmat = post_ref[0]
    lane = lax.broadcasted_iota(jnp.int32, posmat.shape, 1)
    cap = y_refs[0].shape[0]
    kw = min(cap, 256)
    total = None
    spill = []
    for k, y_ref in enumerate(y_refs):
        e = eg * experts_per_step + k
        pos = jnp.sum(jnp.where(lane == e, posmat, 0), axis=1, keepdims=True)
        hi = jnp.max(pos)
        lo = jnp.min(jnp.where(pos >= 0, pos, cap))
        start = pl.multiple_of(jnp.minimum((lo // 16) * 16, cap - kw), 16)
        slot = lax.broadcasted_iota(jnp.int32, (pos.shape[0], kw), 1) + start
        onehot = jnp.where(pos == slot, 1.0, 0.0).astype(BF16)
        part = jnp.dot(onehot, y_ref[pl.ds(start, kw), :], preferred_element_type=F32)
        total = part if total is None else total + part
        spill.append((pos, hi, start, y_ref))
    acc_ref[...] += total

    for pos, hi, start, y_ref in spill:
        @pl.when(hi >= start + kw)
        def _(pos=pos, start=start, y_ref=y_ref):
            slot = lax.broadcasted_iota(jnp.int32, (pos.shape[0], cap), 1)
            onehot = jnp.where((pos == slot) & (slot >= start + kw), 1.0, 0.0).astype(BF16)
            acc_ref[...] += jnp.dot(onehot, y_ref[...], preferred_element_type=F32)

    @pl.when(eg == pl.num_programs(2) - 1)
    def _():
        o_ref[...] = x1_ref[...] + g2_ref[0] * (_rms(acc_ref[...]) * gpost_ref[...])


def _combine(pos_t, y, x1, gate2, gpost, b, s, cap):
    t, d = x1.shape
    n_e = y.shape[0] // (b * cap)
    tt = min(s, 512)
    tiles_per_seq = s // tt
    eps = 4
    assert n_e % eps == 0
    y_specs = [pl.BlockSpec((cap, d), lambda bi, ji, eg, k=k: ((eg * eps + k) * b + bi, 0)) for k in range(eps)]
    return pl.pallas_call(
        functools.partial(_combine_kernel, experts_per_step=eps),
        grid=(b, tiles_per_seq, n_e // eps),
        in_specs=[pl.BlockSpec((1, tt, LANE), lambda bi, ji, eg: (bi, ji, 0))] + y_specs + [
            pl.BlockSpec((tt, d), lambda bi, ji, eg: (bi * tiles_per_seq + ji, 0)),
            pl.BlockSpec((1, 1, d), lambda bi, ji, eg: (bi, 0, 0)),
            pl.BlockSpec((1, d), lambda bi, ji, eg: (0, 0))],
        out_specs=pl.BlockSpec((tt, d), lambda bi, ji, eg: (bi * tiles_per_seq + ji, 0)),
        out_shape=jax.ShapeDtypeStruct((t, d), F32),
        scratch_shapes=[pltpu.VMEM((tt, d), F32)],
        compiler_params=_cparams(("arbitrary", "arbitrary", "arbitrary")),
        name="combine",
    )(pos_t, *([y] * eps), x1, gate2, gpost)


def kernel(x, c, positions, w_ada, b_ada, g_pre_mix, g_post_mix, w_in, diff_lambda_q1, diff_lambda_k1,
           diff_lambda_q2, diff_lambda_k2, diff_subln_gain, ret_decay_logit, ret_norm_gain, w_out, g_pre_ffn,
           g_post_ffn, w_router, w_gate, w_up, w_down):
    b, s, d = x.shape
    t = b * s
    depth = w_ada.shape[0]
    n_e = w_router.shape[2]
    cap = EC_CAPACITY_FACTOR * s // n_e
    diff_w = DIFF_HEADS * LANE

    pos_col = positions.reshape(t, 1)
    tabs = _rope_tables(pos_col)
    c_pad = jnp.pad(c, ((0, (-b) % 8), (0, 0)))
    x2 = x.reshape(t, d)
    for l in range(depth):
        lam_init = 0.8 - 0.6 * math.exp(-0.3 * l)
        mod = _ada(c_pad, w_ada[l], b_ada[l].reshape(1, -1))[:b]
        shift1, scale1, gate1, shift2, scale2, gate2 = [m.reshape(b, 1, d) for m in jnp.split(mod, 6, axis=-1)]

        proj, rgate = _inproj(x2, scale1, shift1, g_pre_mix[l].reshape(1, d), w_in[l].astype(BF16), tabs, s)
        proj3 = proj.reshape(b, s, -1)
        lams = [v[l].reshape(1, -1) for v in (diff_lambda_q1, diff_lambda_k1, diff_lambda_q2, diff_lambda_k2)]
        ad = _diffattn(proj3, lams, diff_subln_gain[l].reshape(1, -1), lam_init)
        ro = _retention(proj3, rgate.reshape(b, s, -1), ret_decay_logit[l], ret_norm_gain[l].reshape(1, -1))

        w_out_bf = w_out[l].astype(BF16)
        x1, h2, logits_t = _outproj(
            ad.reshape(t, -1), ro.reshape(t, -1), w_out_bf[:diff_w], w_out_bf[diff_w:], x2, gate1, scale2,
            shift2, g_post_mix[l].reshape(1, d), g_pre_ffn[l].reshape(1, d), w_router[l].T, s)

        pos_t, info = _route(logits_t, b, s, cap)
        rows = info[:, :, 0, :].astype(jnp.int32) + (jnp.arange(b, dtype=jnp.int32) * s)[:, None, None]
        rows = jnp.transpose(rows, (1, 0, 2)).reshape(-1)
        gate_col = jnp.transpose(info[:, :, 1, :], (1, 0, 2)).reshape(-1, 1)
        xg = _gather_rows(rows, h2)
        y = _ffn(xg, gate_col, w_gate[l], w_up[l], w_down[l], b * cap)
        x2 = _combine(pos_t, y, x1, gate2, g_post_ffn[l].reshape(1, d), b, s, cap)
    return x2.reshape(b, s, d)
```
